```python
import math
import jax
import jax.numpy as jnp
from jax import lax
import numpy as np

D_MODEL = 2048
BATCH = 4
SEQ = 4096
DEPTH = 4

CTX_LEN = 256
GRID_W = 64

MLA_HEADS = 8
MLA_Q_RANK = 512
MLA_KV_RANK = 512
MLA_NOPE = 128
MLA_ROPE = 64
MLA_V = 128
MLA_WIDTH = MLA_HEADS * MLA_V
ROPE_THETA = 10000.0
Q_BLOCK = 128
ATTN_SCALE = (MLA_NOPE + MLA_ROPE) ** -0.5

RWKV_HEADS = 8
RWKV_HEAD = 64
RWKV_WIDTH = RWKV_HEADS * RWKV_HEAD
RWKV_LORA_W = 32
RWKV_LORA_A = 32
RWKV_LORA_G = 96
RWKV_GN_EPS = 64e-5
RWKV_SPLITS = (RWKV_WIDTH, RWKV_WIDTH, RWKV_WIDTH, 2 * RWKV_LORA_W, 2 * RWKV_LORA_A, RWKV_LORA_G)
RWKV_COLS = 3 * RWKV_WIDTH + 2 * RWKV_LORA_W + 2 * RWKV_LORA_A + RWKV_LORA_G

GDN_HEADS = 4
GDN_HEAD = 128
GDN_WIDTH = GDN_HEADS * GDN_HEAD
GDN_CONV = 5
GDN_CHUNK = 64
GDN_SPLITS = (3 * GDN_WIDTH, GDN_WIDTH, 2 * GDN_HEADS, 2 * GDN_HEADS)
GDN_COLS = 4 * GDN_WIDTH + 4 * GDN_HEADS

IN_SPLITS = (MLA_Q_RANK, MLA_KV_RANK, MLA_ROPE, RWKV_COLS, GDN_COLS)
IN_COLS = MLA_Q_RANK + MLA_KV_RANK + MLA_ROPE + RWKV_COLS + GDN_COLS
D_MIX = MLA_WIDTH + RWKV_WIDTH + GDN_WIDTH

D_FF = -(-(8 * D_MODEL) // (3 * 256)) * 256

kernel_name = 'hybrid_mla_rwkv7_gdn_dit_trunk'


def _split(x, sizes):
    return jnp.split(x, [int(s) for s in np.cumsum(sizes)[:-1]], axis=-1)


def _heads(t, n_heads):
    return t.reshape(t.shape[:-1] + (n_heads, t.shape[-1] // n_heads))


def _layer_norm(x, g, b, eps=1e-5):
    xf = x.astype(jnp.float32)
    mu = jnp.mean(xf, -1, keepdims=True)
    var = jnp.mean(jnp.square(xf - mu), -1, keepdims=True)
    return ((xf - mu) * lax.rsqrt(var + eps) * g + b).astype(x.dtype)


def _rms_norm(x, g, eps=1e-6):
    xf = x.astype(jnp.float32)
    return (xf * lax.rsqrt(jnp.mean(jnp.square(xf), -1, keepdims=True) + eps) * g).astype(x.dtype)


def _l2norm(x):
    return x * lax.rsqrt(jnp.sum(x * x, -1, keepdims=True) + 1e-12)


def _modulation(cvec, w_mod, b_mod):
    return jnp.split(jax.nn.silu(cvec) @ w_mod + b_mod, 6, axis=-1)


def _modulate(x, shift, scale):
    return x * (1.0 + scale) + shift


def _swiglu(h, w_gate, w_up, w_down):
    return (jax.nn.silu(h @ w_gate) * (h @ w_up)) @ w_down


def _axial_rope_table(n_tok):
    rows = n_tok // GRID_W
    row = jnp.repeat(jnp.arange(rows), GRID_W)
    col = jnp.tile(jnp.arange(GRID_W), rows)
    pos = jnp.stack([row, col], axis=-1).astype(jnp.float32)
    n_freq = MLA_ROPE // 4
    inv = ROPE_THETA ** (-jnp.arange(n_freq, dtype=jnp.float32) / n_freq)
    ang = pos[..., None] * inv
    return jnp.cos(ang), jnp.sin(ang)


def _apply_axial_rope(x, cos, sin):
    shp = x.shape
    xr = x.reshape(shp[:-1] + (2, 2, MLA_ROPE // 4))
    x1, x2 = xr[..., 0, :], xr[..., 1, :]
    out = jnp.stack([x1 * cos - x2 * sin, x1 * sin + x2 * cos], axis=-2)
    return out.reshape(shp).astype(x.dtype)


def _mla_q(cq, q_norm, w_uq, cos, sin):
    q = _heads(_rms_norm(cq, q_norm) @ w_uq, MLA_HEADS)
    q_nope, q_rot = q[..., :MLA_NOPE], q[..., MLA_NOPE:]
    if cos is not None:
        q_rot = _apply_axial_rope(q_rot, cos[:, None], sin[:, None])
    return jnp.concatenate([q_nope, q_rot], axis=-1)


def _mla_kv(ckv, kr, kv_norm, w_ukv, cos, sin):
    kv = _heads(_rms_norm(ckv, kv_norm) @ w_ukv, MLA_HEADS)
    k_nope, v = kv[..., :MLA_NOPE], kv[..., MLA_NOPE:]
    if cos is not None:
        kr = _apply_axial_rope(kr, cos, sin)
    k_rot = jnp.broadcast_to(kr[:, :, None, :], k_nope.shape[:-1] + (MLA_ROPE,)).astype(k_nope.dtype)
    return jnp.concatenate([k_nope, k_rot], axis=-1), v


def _attend(q, k, v):
    s = jnp.einsum('bqhd,bkhd->bhqk', q, k, preferred_element_type=jnp.float32) * ATTN_SCALE
    p = jax.nn.softmax(s, axis=-1).astype(v.dtype)
    return jnp.einsum('bhqk,bkhd->bqhd', p, v)


def _mla_latent_attention(q, k_all, v_all):
    B, T = q.shape[:2]
    qb = jnp.moveaxis(q.reshape(B, T // Q_BLOCK, Q_BLOCK, MLA_HEADS, q.shape[-1]), 1, 0)
    o = lax.map(lambda qi: _attend(qi, k_all, v_all), qb)
    return jnp.moveaxis(o, 0, 1).reshape(B, T, MLA_WIDTH)


def _bidirectional(run_dir, prep_ctx, prep_lat, s0):
    y_lat, y_ctx = [], []
    for d, reverse in ((0, False), (1, True)):
        yc, s_ctx = run_dir(prep_ctx, d, s0, reverse)
        yl, _ = run_dir(prep_lat, d, s_ctx, reverse)
        y_lat.append(yl)
        y_ctx.append(yc)
    return y_lat[0] + y_lat[1], y_ctx[0] + y_ctx[1]


def _bidir_shift(p):
    z = jnp.zeros_like(p[:, :1])
    return 0.5 * (jnp.concatenate([z, p[:, :-1]], axis=1) + jnp.concatenate([p[:, 1:], z], axis=1))


def _rwkv7_prep(p, mu, w0, w2, a0, a2, k_k, k_a):
    B, T = p.shape[:2]
    p = p + (_bidir_shift(p) - p) * mu
    r, k, v, wd, ad, gd = _split(p, RWKV_SPLITS)
    r, k, v = r.astype(jnp.float32), k.astype(jnp.float32), v.astype(jnp.float32)
    wd = wd.reshape(B, T, 2, RWKV_LORA_W).astype(jnp.float32)
    ad = ad.reshape(B, T, 2, RWKV_LORA_A).astype(jnp.float32)
    w_log = -jax.nn.softplus(-(w0 + jnp.einsum('btdr,drc->btdc', jnp.tanh(wd), w2))) - 0.5
    decay = jnp.exp(-jnp.exp(w_log))
    a_lr = jax.nn.sigmoid(a0 + jnp.einsum('btdr,drc->btdc', ad, a2))
    kk = _l2norm(_heads(k * k_k, RWKV_HEADS))
    k_dir = k[:, :, None, :] * (1.0 + (a_lr - 1.0) * k_a)
    return r, v, kk, decay, a_lr, k_dir, gd


def _wkv7_scan(s0, r, w, k, v, a, b):
    xs = tuple(jnp.moveaxis(t, 1, 0) for t in (r, w, k, v, a, b))

    def step(S, inp):
        r_t, w_t, k_t, v_t, a_t, b_t = inp
        sa = jnp.einsum('bhvk,bhk->bhv', S, a_t)
        S = S * w_t[:, :, None, :] + sa[..., None] * b_t[:, :, None, :] + v_t[..., None] * k_t[:, :, None, :]
        return S, jnp.einsum('bhvk,bhk->bhv', S, r_t)

    S, ys = lax.scan(step, s0, xs)
    return jnp.moveaxis(ys, 0, 1), S


def _rwkv7_dir(prep, d, s0, reverse):
    r, v, kk, decay, a_lr, k_dir, _ = prep
    hd = lambda t: _heads(t, RWKV_HEADS)
    seqs = (hd(r), hd(decay[:, :, d]), hd(k_dir[:, :, d]), hd(v), -kk, kk * hd(a_lr[:, :, d]))
    if reverse:
        seqs = tuple(jnp.flip(t, axis=1) for t in seqs)
    y, s = _wkv7_scan(s0, *seqs)
    if reverse:
        y = jnp.flip(y, axis=1)
    return y, s


def _rwkv7_out(prep, y, r_k, g2, gn_g, gn_b):
    r, v, kk, decay, a_lr, k_dir, gd = prep
    B, T = r.shape[:2]
    mu = jnp.mean(y, -1, keepdims=True)
    var = jnp.mean(jnp.square(y - mu), -1, keepdims=True)
    y = ((y - mu) * lax.rsqrt(var + RWKV_GN_EPS)).reshape(B, T, RWKV_WIDTH) * gn_g + gn_b
    bonus = jnp.einsum('bthn,btdhn,hn->bth', _heads(r, RWKV_HEADS),
                       k_dir.reshape(B, T, 2, RWKV_HEADS, RWKV_HEAD), r_k)[..., None] * _heads(v, RWKV_HEADS)
    g = jax.nn.sigmoid(gd) @ g2
    return (y + bonus.reshape(B, T, RWKV_WIDTH)) * g


def _dwconv_centred(x, w):
    ch = x.shape[-1]
    return lax.conv_general_dilated(x, w.astype(x.dtype)[:, None, :], window_strides=(1,),
                                    padding=[(GDN_CONV // 2, GDN_CONV // 2)],
                                    dimension_numbers=('NWC', 'WIO', 'NWC'), feature_group_count=ch)


def _gdn_prep(p, conv_w, a_log, dt_bias):
    B, T = p.shape[:2]
    qkv, z, a_raw, b_raw = _split(p, GDN_SPLITS)
    qkv = jax.nn.silu(_dwconv_centred(qkv, conv_w))
    q, k, v = jnp.split(qkv.astype(jnp.float32), 3, axis=-1)
    q = _l2norm(_heads(q, GDN_HEADS))
    k = _l2norm(_heads(k, GDN_HEADS))
    v = _heads(v, GDN_HEADS)
    a_raw = a_raw.reshape(B, T, 2, GDN_HEADS).astype(jnp.float32)
    g = -jnp.exp(a_log) * jax.nn.softplus(a_raw + dt_bias)
    beta = jax.nn.sigmoid(b_raw.reshape(B, T, 2, GDN_HEADS).astype(jnp.float32))
    return q, k, v, g, beta, z


def _gated_delta_chunked(s0, q, k, v, g, beta):
    B, T, H, DK = q.shape
    DV = v.shape[-1]
    n = T // GDN_CHUNK

    def to_chunks(t):
        return jnp.moveaxis(t.reshape((B, n, GDN_CHUNK, H) + t.shape[3:]), 3, 1)

    q = to_chunks(q * DK ** -0.5)
    k = to_chunks(k)
    v = to_chunks(v)
    g = jnp.cumsum(to_chunks(g), axis=-1)
    beta = to_chunks(beta)
    idx = jnp.arange(GDN_CHUNK)
    incl = idx[:, None] >= idx[None, :]
    strict = idx[:, None] > idx[None, :]
    decay = jnp.exp(jnp.where(incl, g[..., :, None] - g[..., None, :], -jnp.inf))
    kb = k * beta[..., None]
    m = jnp.where(strict, jnp.einsum('bhnid,bhnjd->bhnij', kb, k) * decay, 0.0)
    eye = jnp.eye(GDN_CHUNK, dtype=m.dtype)
    rhs = jnp.concatenate([v * beta[..., None], kb * jnp.exp(g)[..., None]], axis=-1)
    sol = lax.linalg.triangular_solve(m + eye, rhs, left_side=True, lower=True, unit_diagonal=True)
    u, w = sol[..., :DV], sol[..., DV:]
    a_intra = jnp.where(incl, jnp.einsum('bhnid,bhnjd->bhnij', q, k) * decay, 0.0)
    xs = tuple(jnp.moveaxis(t, 2, 0) for t in (q, k, u, w, g, a_intra))

    def step(S, inp):
        q_i, k_i, u_i, w_i, g_i, a_i = inp
        v_new = u_i - jnp.einsum('bhck,bhkv->bhcv', w_i, S)
        o = (jnp.einsum('bhck,bhkv->bhcv', q_i * jnp.exp(g_i)[..., None], S)
             + jnp.einsum('bhij,bhjv->bhiv', a_i, v_new))
        g_end = g_i[..., -1:]
        S = S * jnp.exp(g_end)[..., None] + jnp.einsum(
            'bhck,bhcv->bhkv', k_i * jnp.exp(g_end - g_i)[..., None], v_new)
        return S, o

    S, o = lax.scan(step, s0, xs)
    o = jnp.moveaxis(o, 0, 2).reshape(B, H, T, DV)
    return jnp.moveaxis(o, 1, 2), S


def _gdn_dir(prep, d, s0, reverse):
    q, k, v, g, beta, _ = prep
    seqs = (q, k, v, g[:, :, d], beta[:, :, d])
    if reverse:
        seqs = tuple(jnp.flip(t, axis=1) for t in seqs)
    o, s = _gated_delta_chunked(s0, *seqs)
    if reverse:
        o = jnp.flip(o, axis=1)
    return o, s


def _gdn_out(prep, o, norm_g):
    z = prep[5]
    B, T = o.shape[:2]
    o = _rms_norm(o, norm_g) * jax.nn.silu(_heads(z.astype(jnp.float32), GDN_HEADS))
    return o.reshape(B, T, GDN_WIDTH)


def _token_mixers(p_lat, p_ctx, cos, sin, mla_q_norm, mla_kv_norm, mla_w_uq, mla_w_ukv,
                  rwkv_mu, rwkv_w0, rwkv_w2, rwkv_a0, rwkv_a2, rwkv_g2, rwkv_k_k, rwkv_k_a, rwkv_r_k,
                  rwkv_gn_g, rwkv_gn_b, gdn_conv, gdn_a_log, gdn_dt_bias, gdn_norm, ctx_out):
    B, n_ctx = p_ctx.shape[:2]
    cq_l, ckv_l, kr_l, pr_l, pg_l = _split(p_lat, IN_SPLITS)
    cq_c, ckv_c, kr_c, pr_c, pg_c = _split(p_ctx, IN_SPLITS)

    k_c, v_c = _mla_kv(ckv_c, kr_c, mla_kv_norm, mla_w_ukv, None, None)
    k_l, v_l = _mla_kv(ckv_l, kr_l, mla_kv_norm, mla_w_ukv, cos, sin)
    q_l = _mla_q(cq_l, mla_q_norm, mla_w_uq, cos, sin)
    a_lat = _mla_latent_attention(q_l, jnp.concatenate([k_c, k_l], axis=1), jnp.concatenate([v_c, v_l], axis=1))

    prep_rc = _rwkv7_prep(pr_c, rwkv_mu, rwkv_w0, rwkv_w2, rwkv_a0, rwkv_a2, rwkv_k_k, rwkv_k_a)
    prep_rl = _rwkv7_prep(pr_l, rwkv_mu, rwkv_w0, rwkv_w2, rwkv_a0, rwkv_a2, rwkv_k_k, rwkv_k_a)
    s0_r = jnp.zeros((B, RWKV_HEADS, RWKV_HEAD, RWKV_HEAD), jnp.float32)
    y_rl, y_rc = _bidirectional(_rwkv7_dir, prep_rc, prep_rl, s0_r)
    b_lat = _rwkv7_out(prep_rl, y_rl, rwkv_r_k, rwkv_g2, rwkv_gn_g, rwkv_gn_b)

    prep_gc = _gdn_prep(pg_c, gdn_conv, gdn_a_log, gdn_dt_bias)
    prep_gl = _gdn_prep(pg_l, gdn_conv, gdn_a_log, gdn_dt_bias)
    s0_g = jnp.zeros((B, GDN_HEADS, GDN_HEAD, GDN_HEAD), jnp.float32)
    o_gl, o_gc = _bidirectional(_gdn_dir, prep_gc, prep_gl, s0_g)
    c_lat = _gdn_out(prep_gl, o_gl, gdn_norm)

    dt = p_lat.dtype
    mix_lat = jnp.concatenate([a_lat.astype(dt), b_lat.astype(dt), c_lat.astype(dt)], axis=-1)
    if not ctx_out:
        return mix_lat, None
    q_c = _mla_q(cq_c, mla_q_norm, mla_w_uq, None, None)
    a_ctx = _attend(q_c, k_c, v_c).reshape(B, n_ctx, MLA_WIDTH)
    b_ctx = _rwkv7_out(prep_rc, y_rc, rwkv_r_k, rwkv_g2, rwkv_gn_g, rwkv_gn_b)
    c_ctx_out = _gdn_out(prep_gc, o_gc, gdn_norm)
    mix_ctx = jnp.concatenate([a_ctx.astype(dt), b_ctx.astype(dt), c_ctx_out.astype(dt)], axis=-1)
    return mix_lat, mix_ctx


def setup_inputs(seed: int = 0) -> dict:
    key = jax.random.key(seed)
    keys = jax.random.split(key, 40)
    counter = [0]

    def nk():
        counter[0] += 1
        return keys[counter[0] - 1]

    def nrm(shape, std):
        return std * jax.random.normal(nk(), shape, jnp.float32)

    L, D = DEPTH, D_MODEL
    beta_init = (8.0 * DEPTH) ** -0.25
    inp = {}
    inp['x'] = nrm((BATCH, SEQ, D), 1.0)
    inp['c'] = nrm((BATCH, D), 1.0)
    inp['ctx'] = nrm((BATCH, CTX_LEN, D), 1.0)
    inp['c_ctx'] = nrm((D,), 1.0)
    inp['w_mod'] = nrm((L, D, 6 * D), 0.5 * D ** -0.5)
    inp['b_mod'] = nrm((L, 6 * D), 0.02)
    inp['w_in'] = nrm((L, D, IN_COLS), D ** -0.5)
    inp['mla_q_norm'] = 1.0 + nrm((L, MLA_Q_RANK), 0.02)
    inp['mla_kv_norm'] = 1.0 + nrm((L, MLA_KV_RANK), 0.02)
    inp['mla_w_uq'] = nrm((L, MLA_Q_RANK, MLA_HEADS * (MLA_NOPE + MLA_ROPE)), MLA_Q_RANK ** -0.5)
    inp['mla_w_ukv'] = nrm((L, MLA_KV_RANK, MLA_HEADS * (MLA_NOPE + MLA_V)), MLA_KV_RANK ** -0.5)
    inp['rwkv_mu'] = jax.random.uniform(nk(), (L, RWKV_COLS), jnp.float32)
    inp['rwkv_w0'] = jax.random.uniform(nk(), (L, 2, RWKV_WIDTH), jnp.float32, -6.0, -1.0)
    inp['rwkv_w2'] = nrm((L, 2, RWKV_LORA_W, RWKV_WIDTH), 0.1 * RWKV_LORA_W ** -0.5)
    inp['rwkv_a0'] = nrm((L, 2, RWKV_WIDTH), 0.1)
    inp['rwkv_a2'] = nrm((L, 2, RWKV_LORA_A, RWKV_WIDTH), RWKV_LORA_A ** -0.5)
    inp['rwkv_g2'] = nrm((L, RWKV_LORA_G, RWKV_WIDTH), RWKV_LORA_G ** -0.5)
    inp['rwkv_k_k'] = 0.85 + nrm((L, RWKV_WIDTH), 0.02)
    inp['rwkv_k_a'] = 1.0 + nrm((L, RWKV_WIDTH), 0.02)
    inp['rwkv_r_k'] = nrm((L, RWKV_HEADS, RWKV_HEAD), 0.1)
    inp['rwkv_gn_g'] = 1.0 + nrm((L, RWKV_WIDTH), 0.02)
    inp['rwkv_gn_b'] = nrm((L, RWKV_WIDTH), 0.02)
    inp['gdn_conv'] = nrm((L, GDN_CONV, 3 * GDN_WIDTH), GDN_CONV ** -0.5)
    inp['gdn_a_log'] = jnp.log(jax.random.uniform(nk(), (L, 2, GDN_HEADS), jnp.float32, 1.0, 16.0))
    dt = jnp.exp(jax.random.uniform(nk(), (L, 2, GDN_HEADS), jnp.float32, math.log(1e-3), math.log(1e-1)))
    inp['gdn_dt_bias'] = dt + jnp.log(-jnp.expm1(-dt))
    inp['gdn_norm'] = 1.0 + nrm((L, GDN_HEAD), 0.02)
    inp['w_out'] = nrm((L, D_MIX, D), beta_init * D_MIX ** -0.5)
    inp['ln1_g'] = 1.0 + nrm((L, D), 0.02)
    inp['ln1_b'] = nrm((L, D), 0.02)
    inp['ffn_w_gate'] = nrm((L, D, D_FF), D ** -0.5)
    inp['ffn_w_up'] = nrm((L, D, D_FF), D ** -0.5)
    inp['ffn_w_down'] = nrm((L, D_FF, D), beta_init * D_FF ** -0.5)
    inp['ln2_g'] = 1.0 + nrm((L, D), 0.02)
    inp['ln2_b'] = nrm((L, D), 0.02)
    return inp


def reference(x, c, ctx, c_ctx, w_mod, b_mod, w_in, mla_q_norm, mla_kv_norm, mla_w_uq, mla_w_ukv,
              rwkv_mu, rwkv_w0, rwkv_w2, rwkv_a0, rwkv_a2, rwkv_g2, rwkv_k_k, rwkv_k_a, rwkv_r_k,
              rwkv_gn_g, rwkv_gn_b, gdn_conv, gdn_a_log, gdn_dt_bias, gdn_norm, w_out, ln1_g, ln1_b,
              ffn_w_gate, ffn_w_up, ffn_w_down, ln2_g, ln2_b):
    alpha = (2.0 * DEPTH) ** 0.25
    cos, sin = _axial_rope_table(x.shape[1])
    cx = ctx
    for l in range(DEPTH):
        ctx_out = l < DEPTH - 1
        sh_m, sc_m, gt_m, sh_f, sc_f, gt_f = [t[:, None, :] for t in _modulation(c, w_mod[l], b_mod[l])]
        csh_m, csc_m, cgt_m, csh_f, csc_f, cgt_f = _modulation(c_ctx, w_mod[l], b_mod[l])
        p_lat = _modulate(x, sh_m, sc_m) @ w_in[l]
        p_ctx = _modulate(cx, csh_m, csc_m) @ w_in[l]
        mix_lat, mix_ctx = _token_mixers(
            p_lat, p_ctx, cos, sin, mla_q_norm[l], mla_kv_norm[l], mla_w_uq[l], mla_w_ukv[l],
            rwkv_mu[l], rwkv_w0[l], rwkv_w2[l], rwkv_a0[l], rwkv_a2[l], rwkv_g2[l], rwkv_k_k[l], rwkv_k_a[l],
            rwkv_r_k[l], rwkv_gn_g[l], rwkv_gn_b[l], gdn_conv[l], gdn_a_log[l], gdn_dt_bias[l], gdn_norm[l],
            ctx_out)
        x = _layer_norm(alpha * x + gt_m * (mix_lat @ w_out[l]), ln1_g[l], ln1_b[l])
        ffn = _swiglu(_modulate(x, sh_f, sc_f), ffn_w_gate[l], ffn_w_up[l], ffn_w_down[l])
        x = _layer_norm(alpha * x + gt_f * ffn, ln2_g[l], ln2_b[l])
        if ctx_out:
            cx = _layer_norm(alpha * cx + cgt_m * (mix_ctx @ w_out[l]), ln1_g[l], ln1_b[l])
            cffn = _swiglu(_modulate(cx, csh_f, csc_f), ffn_w_gate[l], ffn_w_up[l], ffn_w_down[l])
            cx = _layer_norm(alpha * cx + cgt_f * cffn, ln2_g[l], ln2_b[l])
    return x
```

```python
import functools
import math

import numpy as np
import jax
import jax.numpy as jnp
from jax import lax
from jax.experimental import pallas as pl
from jax.experimental.pallas import tpu as pltpu

F32 = jnp.float32
BF16 = jnp.bfloat16
HIGHEST = lax.Precision.HIGHEST

LANES = 128
GRID_W = 64
ROPE_THETA = 10000.0

MLA_HEADS = 8
MLA_RANK = 512
MLA_NOPE = 128
MLA_ROPE = 64
MLA_V = 128
MLA_QK_PAD = 256
ATTN_SCALE = (MLA_NOPE + MLA_ROPE) ** -0.5
ATTN_TQ = 256
ATTN_TK = 512

RWKV_HEADS = 8
RWKV_HEAD = 64
RWKV_WIDTH = RWKV_HEADS * RWKV_HEAD
RWKV_LORA_W = 32
RWKV_LORA_A = 32
RWKV_LORA_G = 96
RWKV_GN_EPS = 64e-5
RWKV_CHUNK = 64
RWKV_GROUP = 1792

GDN_HEADS = 4
GDN_HEAD = 128
GDN_WIDTH = GDN_HEADS * GDN_HEAD
GDN_CONV = 5
GDN_CHUNK = 128
GDN_GROUP = 2176

IN_COLS_PAD = 5120
VMEM_LIMIT = 56 * 1024 * 1024


def _dot(a, b):
    return jnp.dot(a, b, preferred_element_type=F32)


def _dot_nt(a, b):
    return lax.dot_general(a, b, (((1,), (1,)), ((), ())), preferred_element_type=F32)


def _dot_tn(a, b):
    return lax.dot_general(a, b, (((0,), (0,)), ((), ())), preferred_element_type=F32)


def _layer_norm(z, g, b):
    mu = jnp.mean(z, axis=-1, keepdims=True)
    zc = z - mu
    var = jnp.mean(zc * zc, axis=-1, keepdims=True)
    return zc * lax.rsqrt(var + 1e-5) * g + b


def _params(*sem):
    return pltpu.CompilerParams(dimension_semantics=sem, vmem_limit_bytes=VMEM_LIMIT)


def _mod_kernel(c_ref, w_ref, b_ref, o_ref):
    cv = c_ref[...]
    s = cv * jax.nn.sigmoid(cv)
    o_ref[0] = _dot(s.astype(BF16), w_ref[0].astype(BF16)) + b_ref[0]


def _modulation_all(cvec, w_mod, b_mod):
    L, D, N = w_mod.shape
    tn = 1024 if N % 1024 == 0 else 512
    return pl.pallas_call(
        _mod_kernel,
        grid=(L, N // tn),
        in_specs=[pl.BlockSpec((8, D), lambda l, j: (0, 0)),
                  pl.BlockSpec((1, D, tn), lambda l, j: (l, 0, j)),
                  pl.BlockSpec((1, 1, tn), lambda l, j: (l, 0, j))],
        out_specs=pl.BlockSpec((1, 8, tn), lambda l, j: (l, 0, j)),
        out_shape=jax.ShapeDtypeStruct((L, 8, N), F32),
        compiler_params=_params("parallel", "parallel"),
        name="adaln_modulation",
    )(cvec, w_mod, b_mod.reshape(L, 1, N))


def _inproj_kernel(x_ref, mod_ref, w_ref, o_ref):
    m = mod_ref[0]
    xm = x_ref[...] * (1.0 + m[1:2]) + m[0:1]
    o_ref[...] = _dot(xm.astype(BF16), w_ref[...])


def _in_proj(h, mod, w, n_lat, t_len, n_batch, tm):
    ntot, D = h.shape
    ncol = w.shape[1]
    tn = 1024

    def mod_row(i):
        return jnp.where(i * tm < n_lat, (i * tm) // t_len, n_batch)

    return pl.pallas_call(
        _inproj_kernel,
        grid=(ncol // tn, ntot // tm),
        in_specs=[pl.BlockSpec((tm, D), lambda j, i: (i, 0)),
                  pl.BlockSpec((1, 8, D), lambda j, i: (mod_row(i), 0, 0)),
                  pl.BlockSpec((D, tn), lambda j, i: (0, j))],
        out_specs=pl.BlockSpec((tm, tn), lambda j, i: (i, j)),
        out_shape=jax.ShapeDtypeStruct((ntot, ncol), F32),
        compiler_params=_params("parallel", "parallel"),
        name="in_proj",
    )(h, mod, w)


def _mla_proj_kernel(cq_ref, ckv_ref, kr_ref, qn_ref, kn_ref, wq_ref, wkv_ref, cs1_ref, cs2_ref,
                     q_ref, k_ref, v_ref):
    cs1 = cs1_ref[...]
    cs2 = cs2_ref[...]

    def rms(x, g):
        return (x * lax.rsqrt(jnp.mean(x * x, axis=-1, keepdims=True) + 1e-6) * g).astype(BF16)

    def rope(g):
        return g * cs1 + pltpu.roll(g, MLA_ROPE, 1) * cs2

    oq = _dot(rms(cq_ref[...], qn_ref[...]), wq_ref[...])
    okv = _dot(rms(ckv_ref[...], kn_ref[...]), wkv_ref[...])
    krot = rope(kr_ref[...])
    for hd in range(MLA_HEADS):
        c0 = hd * MLA_QK_PAD
        qn = oq[:, c0:c0 + MLA_NOPE]
        qr = rope(oq[:, c0 + MLA_NOPE:c0 + MLA_QK_PAD])
        q_ref[hd] = (jnp.concatenate([qn, qr], axis=1) * ATTN_SCALE).astype(BF16)
        k_ref[hd] = jnp.concatenate([okv[:, c0:c0 + MLA_NOPE], krot], axis=1).astype(BF16)
        v_ref[hd] = okv[:, c0 + MLA_NOPE:c0 + MLA_QK_PAD].astype(BF16)


def _mla_proj(p, q_norm, kv_norm, wq, wkv, cs1, cs2, n_lat, t_len, tm):
    ntot = p.shape[0]
    H = MLA_HEADS
    npos = t_len // tm

    def pos(i):
        return jnp.where(i * tm < n_lat, i % npos, npos)

    return pl.pallas_call(
        _mla_proj_kernel,
        grid=(ntot // tm,),
        in_specs=[pl.BlockSpec((tm, MLA_RANK), lambda i: (i, 0)),
                  pl.BlockSpec((tm, MLA_RANK), lambda i: (i, 1)),
                  pl.BlockSpec((tm, LANES), lambda i: (i, 2 * MLA_RANK // LANES)),
                  pl.BlockSpec((1, MLA_RANK), lambda i: (0, 0)),
                  pl.BlockSpec((1, MLA_RANK), lambda i: (0, 0)),
                  pl.BlockSpec(wq.shape, lambda i: (0, 0)),
                  pl.BlockSpec(wkv.shape, lambda i: (0, 0)),
                  pl.BlockSpec((tm, LANES), lambda i: (pos(i), 0)),
                  pl.BlockSpec((tm, LANES), lambda i: (pos(i), 0))],
        out_specs=[pl.BlockSpec((H, tm, MLA_QK_PAD), lambda i: (0, i, 0)),
                   pl.BlockSpec((H, tm, MLA_QK_PAD), lambda i: (0, i, 0)),
                   pl.BlockSpec((H, tm, MLA_V), lambda i: (0, i, 0))],
        out_shape=[jax.ShapeDtypeStruct((H, ntot, MLA_QK_PAD), BF16),
                   jax.ShapeDtypeStruct((H, ntot, MLA_QK_PAD), BF16),
                   jax.ShapeDtypeStruct((H, ntot, MLA_V), BF16)],
        compiler_params=_params("parallel"),
        name="mla_proj",
    )(p, p, p, q_norm, kv_norm, wq, wkv, cs1, cs2)


def _attn_kernel(q_ref, kl_ref, kc_ref, vl_ref, vc_ref, o_ref, *, nq_lat, n_chunks):
    qi = pl.program_id(2)
    q = q_ref[0]
    s = _dot_nt(q, kc_ref[0])
    m = jnp.max(s, axis=-1, keepdims=True)
    p = jnp.exp(s - m)
    l = jnp.sum(p, axis=-1, keepdims=True)
    acc = _dot(p.astype(BF16), vc_ref[0])

    def body(c, carry):
        m, l, acc = carry
        off = pl.multiple_of(c * ATTN_TK, ATTN_TK)
        s = _dot_nt(q, kl_ref[0, pl.ds(off, ATTN_TK), :])
        m_new = jnp.maximum(m, jnp.max(s, axis=-1, keepdims=True))
        scale = jnp.exp(m - m_new)
        p = jnp.exp(s - m_new)
        l = l * scale + jnp.sum(p, axis=-1, keepdims=True)
        acc = acc * scale + _dot(p.astype(BF16), vl_ref[0, pl.ds(off, ATTN_TK), :])
        return m_new, l, acc

    trips = jnp.where(qi < nq_lat, n_chunks, 0)
    m, l, acc = lax.fori_loop(0, trips, body, (m, l, acc))
    o_ref[...] = (acc / l).astype(o_ref.dtype)


def _attention(qh, kh, vh, n_batch, t_len, ctx_len, with_ctx):
    H, ntot, _ = qh.shape
    n_lat = n_batch * t_len
    tq = ATTN_TQ
    assert ctx_len == tq and t_len % ATTN_TK == 0
    nq_lat = t_len // tq
    nq = nq_lat + (1 if with_ctx else 0)
    ctx_blk0 = n_lat // ctx_len

    def qrow(b, qi):
        return jnp.where(qi < nq_lat, b * nq_lat + qi, n_lat // tq + b)

    kern = functools.partial(_attn_kernel, nq_lat=nq_lat, n_chunks=t_len // ATTN_TK)
    return pl.pallas_call(
        kern,
        grid=(n_batch, H, nq),
        in_specs=[pl.BlockSpec((1, tq, MLA_QK_PAD), lambda b, h, qi: (h, qrow(b, qi), 0)),
                  pl.BlockSpec((1, t_len, MLA_QK_PAD), lambda b, h, qi: (h, b, 0)),
                  pl.BlockSpec((1, ctx_len, MLA_QK_PAD), lambda b, h, qi: (h, ctx_blk0 + b, 0)),
                  pl.BlockSpec((1, t_len, MLA_V), lambda b, h, qi: (h, b, 0)),
                  pl.BlockSpec((1, ctx_len, MLA_V), lambda b, h, qi: (h, ctx_blk0 + b, 0))],
        out_specs=pl.BlockSpec((tq, MLA_V), lambda b, h, qi: (qrow(b, qi), h)),
        out_shape=jax.ShapeDtypeStruct((ntot, H * MLA_V), BF16),
        compiler_params=_params("parallel", "parallel", "arbitrary"),
        name="mla_attention",
    )(qh, kh, kh, vh, vh)


def _tri(n, reverse, transpose=False):
    i = lax.broadcasted_iota(jnp.int32, (n, n), 0)
    j = lax.broadcasted_iota(jnp.int32, (n, n), 1)
    if transpose:
        i, j = j, i
    return ((j >= i) if reverse else (j <= i)).astype(F32)


def _unit_tri_inverse(nmat, reverse, size):
    n = nmat.shape[0]
    sub = 8
    ii = lax.broadcasted_iota(jnp.int32, (n, n), 0)
    jj = lax.broadcasted_iota(jnp.int32, (n, n), 1)
    x = (ii == jj).astype(F32)
    nd = jnp.where((ii // sub) == (jj // sub), nmat, 0.0)
    jloc = jj % sub
    for j in (range(sub - 1, 0, -1) if reverse else range(sub - 1)):
        col = jnp.sum(jnp.where(jloc == j, nd, 0.0), axis=1, keepdims=True)
        row = jnp.broadcast_to(x.reshape(n // sub, sub, n)[:, j:j + 1, :], (n // sub, sub, n)).reshape(n, n)
        x = x + col * row
    s = sub
    while s < size:
        off = ((ii // (2 * s)) == (jj // (2 * s))) & ((ii // s) != (jj // s))
        xb = x.astype(BF16)
        x = x + _dot(_dot(xb, jnp.where(off, nmat, 0.0).astype(BF16)).astype(BF16), xb)
        s *= 2
    return x


def _rwkv_pair(r, v, a, lw, kd, bd, state, reverse):
    C = RWKV_CHUNK
    cum = jnp.dot(_tri(C, reverse), lw, precision=HIGHEST, preferred_element_type=F32)
    tot = jnp.sum(lw, axis=0, keepdims=True)
    e_in = jnp.exp(cum)
    e_ex = jnp.exp(cum - lw)
    e_neg = jnp.exp(-cum)
    e_rem = jnp.exp(tot - cum)
    lane = lax.broadcasted_iota(jnp.int32, (C, LANES), 1)
    m0 = (lane < RWKV_HEAD).astype(F32)
    m1 = 1.0 - m0

    def ex(x):
        return jnp.concatenate([x * m0, x * m1], axis=0)

    at, rt = ex(a * e_ex), ex(r * e_in)
    bt, kt = ex(bd * e_neg), ex(kd * e_neg)
    bh, kh = ex(bd * e_rem), ex(kd * e_rem)
    ve = ex(v)

    n2 = 2 * C
    aa = _dot_nt(jnp.concatenate([at, rt], axis=0).astype(BF16), jnp.concatenate([bt, kt], axis=0).astype(BF16))
    ii = lax.broadcasted_iota(jnp.int32, (n2, n2), 0)
    jj = lax.broadcasted_iota(jnp.int32, (n2, n2), 1)
    same = (ii // C) == (jj // C)
    il, jl = ii % C, jj % C
    before = (jl > il) if reverse else (jl < il)
    strict = same & before
    incl = same & (before | (il == jl))
    a_ab = jnp.where(strict, aa[:n2, :n2], 0.0)
    a_ak = jnp.where(strict, aa[:n2, n2:], 0.0)
    a_rb = jnp.where(incl, aa[n2:, :n2], 0.0)
    a_rk = jnp.where(incl, aa[n2:, n2:], 0.0)

    tinv = _unit_tri_inverse(a_ab, reverse, C)
    veb = ve.astype(BF16)
    akv = _dot(a_ak.astype(BF16), veb)
    z = _dot(tinv.astype(BF16), jnp.concatenate([at, akv], axis=1).astype(BF16))
    zb = z.astype(BF16)
    w2 = _dot(a_rb.astype(BF16), zb)
    rhat = rt + w2[:, :LANES]
    y0 = w2[:, LANES:] + _dot(a_rk.astype(BF16), veb)
    bhb = bh.astype(BF16)
    gmt = _dot_tn(zb[:, :LANES], bhb)
    s0t = _dot_tn(jnp.concatenate([zb[:, LANES:], veb], axis=0), jnp.concatenate([bhb, kh.astype(BF16)], axis=0))
    sb = state.astype(BF16)
    ye = _dot_nt(rhat.astype(BF16), sb) + y0
    y = ye[:C] + ye[C:]
    new_state = state * jnp.exp(tot) + _dot(sb, gmt.astype(BF16)) + s0t
    return y, new_state


def _rwkv_scan_kernel(rf, rb, vf, vb, af, ab, lw0, lw1, kd0, kd1, b0, b1, yf_ref, yb_ref, s_ref):
    @pl.when(pl.program_id(1) == 0)
    def _():
        s_ref[...] = jnp.zeros_like(s_ref)

    dirs = ((rf, vf, af, lw0, kd0, b0, yf_ref), (rb, vb, ab, lw1, kd1, b1, yb_ref))
    for d, (r_, v_, a_, lw_, kd_, b_, y_) in enumerate(dirs):
        for j in range(RWKV_WIDTH // LANES):
            sl = slice(j * LANES, (j + 1) * LANES)
            y, s_new = _rwkv_pair(r_[:, sl], v_[:, sl], a_[:, sl], lw_[:, sl], kd_[:, sl], b_[:, sl],
                                  s_ref[d, j], reverse=(d == 1))
            s_ref[d, j] = s_new
            y_[:, sl] = y


def _chunk_maps(n_batch, t_len, ctx_len, chunk):
    nctx, nlat = ctx_len // chunk, t_len // chunk
    base = n_batch * nlat

    def fwd(b, i):
        return jnp.where(i < nctx, base + b * nctx + i, b * nlat + i - nctx)

    def bwd(b, i):
        return jnp.where(i < nctx, base + b * nctx + (nctx - 1 - i), b * nlat + (nlat - 1 - (i - nctx)))

    return fwd, bwd, nctx + nlat


def _rwkv_scan(r, v, a, lw0, lw1, kd0, kd1, b0, b1, n_batch, t_len, ctx_len):
    ntot, W = r.shape
    C = RWKV_CHUNK
    fwd, bwd, nsteps = _chunk_maps(n_batch, t_len, ctx_len, C)
    fs = pl.BlockSpec((C, W), lambda b, i: (fwd(b, i), 0))
    bs = pl.BlockSpec((C, W), lambda b, i: (bwd(b, i), 0))
    return pl.pallas_call(
        _rwkv_scan_kernel,
        grid=(n_batch, nsteps),
        in_specs=[fs, bs, fs, bs, fs, bs, fs, bs, fs, bs, fs, bs],
        out_specs=[fs, bs],
        out_shape=[jax.ShapeDtypeStruct((ntot, W), F32)] * 2,
        scratch_shapes=[pltpu.VMEM((2, W // LANES, LANES, LANES), F32)],
        compiler_params=_params("parallel", "arbitrary"),
        name="rwkv7_scan",
    )(r, r, v, v, a, a, lw0, lw1, kd0, kd1, b0, b1)


def _gdn_scan_kernel(qf, qb, kf, kb, vf, vb, gf, gb, of_ref, ob_ref, s_ref):
    @pl.when(pl.program_id(1) == 0)
    def _():
        s_ref[...] = jnp.zeros_like(s_ref)

    C = GDN_CHUNK
    ii = lax.broadcasted_iota(jnp.int32, (C, C), 0)
    jj = lax.broadcasted_iota(jnp.int32, (C, C), 1)
    for d, (q_, k_, v_, g_, o_) in enumerate(((qf, kf, vf, gf, of_ref), (qb, kb, vb, gb, ob_ref))):
        reverse = d == 1
        strict = (jj > ii) if reverse else (jj < ii)
        incl = strict | (ii == jj)
        gbv = g_[...]
        gcol = jnp.dot(_tri(C, reverse), gbv, precision=HIGHEST, preferred_element_type=F32)
        grow = jnp.dot(gbv.T, _tri(C, reverse, transpose=True), precision=HIGHEST, preferred_element_type=F32)
        gtot = jnp.sum(gbv, axis=0, keepdims=True)
        for hd in range(GDN_HEADS):
            sl = slice(hd * GDN_HEAD, (hd + 1) * GDN_HEAD)
            cg = d * GDN_HEADS + hd
            cb = 2 * GDN_HEADS + cg
            q, k, v = q_[:, sl], k_[:, sl], v_[:, sl]
            gc = gcol[:, cg:cg + 1]
            gr = grow[cg:cg + 1, :]
            gt = gtot[:, cg:cg + 1]
            beta = gbv[:, cb:cb + 1]
            dec = jnp.exp(jnp.where(incl, gc - gr, -1e30))
            kbeta = k * beta
            kb16 = k.astype(BF16)
            qk = _dot_nt(jnp.concatenate([kbeta, q], axis=0).astype(BF16), kb16)
            nmat = jnp.where(strict, -qk[:C] * dec, 0.0)
            aint = jnp.where(incl, qk[C:] * dec, 0.0)
            tinv = _unit_tri_inverse(nmat, reverse, C)
            eg = jnp.exp(gc)
            sol = _dot(tinv.astype(BF16), jnp.concatenate([v * beta, kbeta * eg], axis=1).astype(BF16))
            solb = sol.astype(BF16)
            qo = _dot(aint.astype(BF16), solb)
            qhat = q * eg - qo[:, GDN_HEAD:]
            khat = (k * jnp.exp(gt - gc)).astype(BF16)
            gs = _dot_tn(khat, solb)
            state = s_ref[d, hd]
            sb = state.astype(BF16)
            o_[:, sl] = _dot(qhat.astype(BF16), sb) + qo[:, :GDN_HEAD]
            s_ref[d, hd] = state * jnp.exp(gt) - _dot(gs[:, GDN_HEAD:].astype(BF16), sb) + gs[:, :GDN_HEAD]


def _gdn_scan(q, k, v, gb, n_batch, t_len, ctx_len):
    ntot, W = q.shape
    C = GDN_CHUNK
    fwd, bwd, nsteps = _chunk_maps(n_batch, t_len, ctx_len, C)
    fs = pl.BlockSpec((C, W), lambda b, i: (fwd(b, i), 0))
    bs = pl.BlockSpec((C, W), lambda b, i: (bwd(b, i), 0))
    gfs = pl.BlockSpec((C, LANES), lambda b, i: (fwd(b, i), 0))
    gbs = pl.BlockSpec((C, LANES), lambda b, i: (bwd(b, i), 0))
    return pl.pallas_call(
        _gdn_scan_kernel,
        grid=(n_batch, nsteps),
        in_specs=[fs, bs, fs, bs, fs, bs, gfs, gbs],
        out_specs=[fs, bs],
        out_shape=[jax.ShapeDtypeStruct((ntot, W), F32)] * 2,
        scratch_shapes=[pltpu.VMEM((2, GDN_HEADS, GDN_HEAD, GDN_HEAD), F32)],
        compiler_params=_params("parallel", "arbitrary"),
        name="gdn_scan",
    )(q, q, k, k, v, v, gb, gb)


def _outproj_kernel(x_ref, a_ref, b_ref, c_ref, w_ref, mod_ref, g_ref, beta_ref, o_ref, *, alpha):
    na, nb = a_ref.shape[1], b_ref.shape[1]
    y = _dot(a_ref[...], w_ref[0:na, :])
    y += _dot(b_ref[...], w_ref[na:na + nb, :])
    y += _dot(c_ref[...], w_ref[na + nb:, :])
    gate = mod_ref[0][2:3]
    o_ref[...] = _layer_norm(alpha * x_ref[...] + gate * y, g_ref[...], beta_ref[...])


def _out_proj(h, a, bm, cm, w, mod, ln_g, ln_b, n_rows, n_lat, t_len, n_batch, tm, alpha):
    D = h.shape[1]

    def mod_row(i):
        return jnp.where(i * tm < n_lat, (i * tm) // t_len, n_batch)

    return pl.pallas_call(
        functools.partial(_outproj_kernel, alpha=alpha),
        grid=(n_rows // tm,),
        in_specs=[pl.BlockSpec((tm, D), lambda i: (i, 0)),
                  pl.BlockSpec((tm, a.shape[1]), lambda i: (i, 0)),
                  pl.BlockSpec((tm, bm.shape[1]), lambda i: (i, 0)),
                  pl.BlockSpec((tm, cm.shape[1]), lambda i: (i, 0)),
                  pl.BlockSpec(w.shape, lambda i: (0, 0)),
                  pl.BlockSpec((1, 8, D), lambda i: (mod_row(i), 0, 0)),
                  pl.BlockSpec((1, D), lambda i: (0, 0)),
                  pl.BlockSpec((1, D), lambda i: (0, 0))],
        out_specs=pl.BlockSpec((tm, D), lambda i: (i, 0)),
        out_shape=jax.ShapeDtypeStruct((n_rows, D), F32),
        compiler_params=_params("parallel"),
        name="out_proj_ln",
    )(h, a, bm, cm, w, mod, ln_g, ln_b)


def _ffn_kernel(x_ref, mod_ref, wg_ref, wu_ref, wd_ref, g_ref, beta_ref, o_ref, xm_ref, acc_ref, *, alpha):
    f = pl.program_id(1)

    @pl.when(f == 0)
    def _():
        m = mod_ref[0]
        xm_ref[...] = (x_ref[...] * (1.0 + m[4:5]) + m[3:4]).astype(BF16)
        acc_ref[...] = jnp.zeros_like(acc_ref)

    xm = xm_ref[...]
    gate = _dot(xm, wg_ref[...])
    up = _dot(xm, wu_ref[...])
    act = gate * jax.nn.sigmoid(gate) * up
    acc_ref[...] += _dot(act.astype(BF16), wd_ref[...])

    @pl.when(f == pl.num_programs(1) - 1)
    def _():
        m = mod_ref[0]
        o_ref[...] = _layer_norm(alpha * x_ref[...] + m[5:6] * acc_ref[...], g_ref[...], beta_ref[...])


def _ffn(h, mod, wg, wu, wd, ln_g, ln_b, n_rows, n_lat, t_len, n_batch, tm, tf, alpha):
    D = h.shape[1]
    F = wg.shape[1]

    def mod_row(i):
        return jnp.where(i * tm < n_lat, (i * tm) // t_len, n_batch)

    return pl.pallas_call(
        functools.partial(_ffn_kernel, alpha=alpha),
        grid=(n_rows // tm, F // tf),
        in_specs=[pl.BlockSpec((tm, D), lambda i, f: (i, 0)),
                  pl.BlockSpec((1, 8, D), lambda i, f: (mod_row(i), 0, 0)),
                  pl.BlockSpec((D, tf), lambda i, f: (0, f)),
                  pl.BlockSpec((D, tf), lambda i, f: (0, f)),
                  pl.BlockSpec((tf, D), lambda i, f: (f, 0)),
                  pl.BlockSpec((1, D), lambda i, f: (0, 0)),
                  pl.BlockSpec((1, D), lambda i, f: (0, 0))],
        out_specs=pl.BlockSpec((tm, D), lambda i, f: (i, 0)),
        out_shape=jax.ShapeDtypeStruct((n_rows, D), F32),
        scratch_shapes=[pltpu.VMEM((tm, D), BF16), pltpu.VMEM((tm, D), F32)],
        compiler_params=_params("parallel", "arbitrary"),
        name="ffn_ln",
    )(h, mod, wg, wu, wd, ln_g, ln_b)


def _in_proj_columns():
    kr0 = 2 * MLA_RANK
    swap = np.concatenate([np.arange(16, 32), np.arange(0, 16), np.arange(48, 64), np.arange(32, 48)])
    pr0 = kr0 + MLA_ROPE
    n_r = 3 * RWKV_WIDTH + 2 * RWKV_LORA_W + 2 * RWKV_LORA_A + RWKV_LORA_G
    pg0 = pr0 + n_r
    n_g = 4 * GDN_WIDTH + 4 * GDN_HEADS
    cols = [np.arange(0, kr0), kr0 + np.arange(MLA_ROPE), kr0 + swap,
            pr0 + np.arange(n_r), -np.ones(RWKV_GROUP - n_r, np.int64),
            pg0 + np.arange(n_g), -np.ones(GDN_GROUP - n_g, np.int64)]
    idx = np.concatenate(cols)
    assert idx.shape[0] == IN_COLS_PAD
    return idx


def _take_cols(w, idx):
    safe = jnp.asarray(np.maximum(idx, 0), jnp.int32)
    mask = jnp.asarray(idx >= 0)
    return jnp.where(mask, jnp.take(w, safe, axis=-1), 0.0)


def _uq_columns():
    swap = np.concatenate([np.arange(16, 32), np.arange(0, 16), np.arange(48, 64), np.arange(32, 48)])
    per = MLA_NOPE + MLA_ROPE
    out = []
    for hd in range(MLA_HEADS):
        out += [hd * per + np.arange(MLA_NOPE), hd * per + MLA_NOPE + np.arange(MLA_ROPE), hd * per + MLA_NOPE + swap]
    return np.concatenate(out)


def _rope_tables(t_len, tm):
    rows = t_len // GRID_W
    row = jnp.repeat(jnp.arange(rows), GRID_W)
    col = jnp.tile(jnp.arange(GRID_W), rows)
    n_freq = MLA_ROPE // 4
    inv = ROPE_THETA ** (-jnp.arange(n_freq, dtype=F32) / n_freq)
    ar = row.astype(F32)[:, None] * inv
    ac = col.astype(F32)[:, None] * inv
    cos = jnp.concatenate([jnp.cos(ar), jnp.cos(ar), jnp.cos(ac), jnp.cos(ac)], axis=1)
    sin = jnp.concatenate([-jnp.sin(ar), jnp.sin(ar), -jnp.sin(ac), jnp.sin(ac)], axis=1)
    zer = jnp.zeros((t_len, LANES - MLA_ROPE), F32)
    cs1 = jnp.concatenate([cos, zer], axis=1)
    cs2 = jnp.concatenate([sin, zer], axis=1)
    ident = jnp.concatenate([jnp.ones((tm, MLA_ROPE), F32), jnp.zeros((tm, LANES - MLA_ROPE), F32)], axis=1)
    return jnp.concatenate([cs1, ident], axis=0), jnp.concatenate([cs2, jnp.zeros((tm, LANES), F32)], axis=0)


def _seq_view(x, n_batch, t_len, ctx_len):
    n_lat = n_batch * t_len
    return x[:n_lat].reshape(n_batch, t_len, -1), x[n_lat:].reshape(n_batch, ctx_len, -1)


def _seq_apply(fn, x, n_batch, t_len, ctx_len):
    lat, cx = _seq_view(x, n_batch, t_len, ctx_len)
    return jnp.concatenate([fn(lat).reshape(n_batch * t_len, -1), fn(cx).reshape(n_batch * ctx_len, -1)], axis=0)


def _rwkv_prep(pr, mu, w0, w2, a0, a2, g2, k_k, k_a, n_batch, t_len, ctx_len):
    def shift(p):
        z = jnp.zeros_like(p[:, :1])
        return 0.5 * (jnp.concatenate([z, p[:, :-1]], axis=1) + jnp.concatenate([p[:, 1:], z], axis=1))

    p = pr + (_seq_apply(shift, pr, n_batch, t_len, ctx_len) - pr) * mu
    W = RWKV_WIDTH
    r, k, v = p[:, :W], p[:, W:2 * W], p[:, 2 * W:3 * W]
    o = 3 * W
    wd = p[:, o:o + 2 * RWKV_LORA_W].reshape(-1, 2, RWKV_LORA_W)
    o += 2 * RWKV_LORA_W
    ad = p[:, o:o + 2 * RWKV_LORA_A].reshape(-1, 2, RWKV_LORA_A)
    o += 2 * RWKV_LORA_A
    gd = p[:, o:o + RWKV_LORA_G]
    w_log = -jax.nn.softplus(-(w0 + jnp.einsum('ndr,drc->ndc', jnp.tanh(wd), w2))) - 0.5
    lw = -jnp.exp(w_log)
    a_lr = jax.nn.sigmoid(a0 + jnp.einsum('ndr,drc->ndc', ad, a2))
    kk = (k * k_k).reshape(-1, RWKV_HEADS, RWKV_HEAD)
    kk = (kk * lax.rsqrt(jnp.sum(kk * kk, -1, keepdims=True) + 1e-12)).reshape(-1, W)
    kd = k[:, None, :] * (1.0 + (a_lr - 1.0) * k_a)
    bd = kk[:, None, :] * a_lr
    g = jax.nn.sigmoid(gd) @ g2
    return r, v, -kk, lw[:, 0], lw[:, 1], kd[:, 0], kd[:, 1], bd[:, 0], bd[:, 1], g


def _rwkv_post(y, r, v, kd0, kd1, g, r_k, gn_g, gn_b):
    n = y.shape[0]
    yh = y.reshape(n, RWKV_HEADS, RWKV_HEAD)
    mu = jnp.mean(yh, -1, keepdims=True)
    var = jnp.mean(jnp.square(yh - mu), -1, keepdims=True)
    yn = ((yh - mu) * lax.rsqrt(var + RWKV_GN_EPS)).reshape(n, RWKV_WIDTH) * gn_g + gn_b
    bonus = jnp.sum((r * (kd0 + kd1)).reshape(n, RWKV_HEADS, RWKV_HEAD) * r_k, -1, keepdims=True)
    bonus = (bonus * v.reshape(n, RWKV_HEADS, RWKV_HEAD)).reshape(n, RWKV_WIDTH)
    return (yn + bonus) * g


def _gdn_prep(pg, conv_w, a_log, dt_bias, n_batch, t_len, ctx_len):
    W = GDN_WIDTH
    qkv = pg[:, :3 * W]
    pad = GDN_CONV // 2

    def conv(xs):
        xp = jnp.pad(xs, ((0, 0), (pad, pad), (0, 0)))
        n = xs.shape[1]
        return sum(xp[:, j:j + n] * conv_w[j] for j in range(GDN_CONV))

    qkv = jax.nn.silu(_seq_apply(conv, qkv, n_batch, t_len, ctx_len))

    def l2h(t):
        th = t.reshape(-1, GDN_HEADS, GDN_HEAD)
        return (th * lax.rsqrt(jnp.sum(th * th, -1, keepdims=True) + 1e-12)).reshape(-1, W)

    q = l2h(qkv[:, :W]) * GDN_HEAD ** -0.5
    k = l2h(qkv[:, W:2 * W])
    v = qkv[:, 2 * W:]
    z = pg[:, 3 * W:4 * W]
    nh2 = 2 * GDN_HEADS
    a_raw = pg[:, 4 * W:4 * W + nh2]
    b_raw = pg[:, 4 * W + nh2:4 * W + 2 * nh2]
    glog = -jnp.exp(a_log.reshape(1, nh2)) * jax.nn.softplus(a_raw + dt_bias.reshape(1, nh2))
    beta = jax.nn.sigmoid(b_raw)
    gb = jnp.concatenate([glog, beta, jnp.zeros((pg.shape[0], LANES - 2 * nh2), F32)], axis=1)
    return q, k, v, gb, z


def _gdn_post(o, z, norm_g):
    n = o.shape[0]
    oh = o.reshape(n, GDN_HEADS, GDN_HEAD)
    oh = oh * lax.rsqrt(jnp.mean(oh * oh, -1, keepdims=True) + 1e-6) * norm_g
    return oh.reshape(n, GDN_WIDTH) * jax.nn.silu(z)


def kernel(x, c, ctx, c_ctx, w_mod, b_mod, w_in, mla_q_norm, mla_kv_norm, mla_w_uq, mla_w_ukv, rwkv_mu, rwkv_w0, rwkv_w2, rwkv_a0, rwkv_a2, rwkv_g2, rwkv_k_k, rwkv_k_a, rwkv_r_k, rwkv_gn_g, rwkv_gn_b, gdn_conv, gdn_a_log, gdn_dt_bias, gdn_norm, w_out, ln1_g, ln1_b, ffn_w_gate, ffn_w_up, ffn_w_down, ln2_g, ln2_b):
    B, T, D = x.shape
    CTX = ctx.shape[1]
    L = w_mod.shape[0]
    n_lat, n_ctx = B * T, B * CTX
    ntot = n_lat + n_ctx
    alpha = (2.0 * L) ** 0.25
    tm = 256
    tm_ffn = 512 if (n_lat % 512 == 0 and n_ctx % 512 == 0 and T % 512 == 0) else tm
    tf = 512
    assert T % tm == 0 and n_ctx % tm == 0 and B + 1 <= 8

    cvec = jnp.concatenate([c, c_ctx[None], jnp.zeros((8 - B - 1, D), F32)], axis=0)
    mod_all = _modulation_all(cvec, w_mod, b_mod).reshape(L, 8, 6, D)[:, :B + 1]
    mod_all = jnp.pad(mod_all, ((0, 0), (0, 0), (0, 2), (0, 0)))

    in_idx = _in_proj_columns()
    w_in_p = _take_cols(w_in, in_idx).astype(BF16)
    uq_idx = _uq_columns()
    w_uq_p = jnp.take(mla_w_uq, jnp.asarray(uq_idx, jnp.int32), axis=-1).astype(BF16)
    w_ukv_b = mla_w_ukv.astype(BF16)
    w_out_b = w_out.astype(BF16)
    wg_b, wu_b, wd_b = ffn_w_gate.astype(BF16), ffn_w_up.astype(BF16), ffn_w_down.astype(BF16)
    n_r = 3 * RWKV_WIDTH + 2 * RWKV_LORA_W + 2 * RWKV_LORA_A + RWKV_LORA_G
    mu_p = jnp.pad(rwkv_mu, ((0, 0), (0, RWKV_GROUP - n_r)))
    cs1, cs2 = _rope_tables(T, tm)

    h = jnp.concatenate([x.reshape(n_lat, D), ctx.reshape(n_ctx, D)], axis=0)
    r0 = 2 * MLA_RANK + LANES
    for l in range(L):
        last = l == L - 1
        mod = mod_all[l]
        p = _in_proj(h, mod, w_in_p[l], n_lat, T, B, tm)

        qh, kh, vh = _mla_proj(p, mla_q_norm[l][None], mla_kv_norm[l][None], w_uq_p[l], w_ukv_b[l], cs1, cs2,
                               n_lat, T, tm)
        a_mix = _attention(qh, kh, vh, B, T, CTX, with_ctx=not last)

        r, v, a, lw0, lw1, kd0, kd1, b0, b1, g = _rwkv_prep(
            p[:, r0:r0 + RWKV_GROUP], mu_p[l], rwkv_w0[l], rwkv_w2[l], rwkv_a0[l], rwkv_a2[l], rwkv_g2[l],
            rwkv_k_k[l], rwkv_k_a[l], B, T, CTX)
        yf, yb = _rwkv_scan(r, v, a, lw0, lw1, kd0, kd1, b0, b1, B, T, CTX)
        b_mix = _rwkv_post(yf + yb, r, v, kd0, kd1, g, rwkv_r_k[l], rwkv_gn_g[l], rwkv_gn_b[l]).astype(BF16)

        gq, gk, gv, gb, gz = _gdn_prep(p[:, r0 + RWKV_GROUP:], gdn_conv[l], gdn_a_log[l], gdn_dt_bias[l], B, T, CTX)
        of, ob = _gdn_scan(gq, gk, gv, gb, B, T, CTX)
        c_mix = _gdn_post(of + ob, gz, gdn_norm[l]).astype(BF16)

        n_rows = n_lat if last else ntot
        h1 = _out_proj(h, a_mix, b_mix, c_mix, w_out_b[l], mod, ln1_g[l][None], ln1_b[l][None],
                       n_rows, n_lat, T, B, tm, alpha)
        h = _ffn(h1, mod, wg_b[l], wu_b[l], wd_b[l], ln2_g[l][None], ln2_b[l][None],
                 n_rows, n_lat, T, B, tm_ffn, tf, alpha)
    return h[:n_lat].reshape(B, T, D)
```

```python
import functools
import math

import numpy as np
import jax
import jax.numpy as jnp
from jax import lax
from jax.experimental import pallas as pl
from jax.experimental.pallas import tpu as pltpu

F32 = jnp.float32
BF16 = jnp.bfloat16
HIGHEST = lax.Precision.HIGHEST

LANES = 128
GRID_W = 64
ROPE_THETA = 10000.0

MLA_HEADS = 8
MLA_RANK = 512
MLA_NOPE = 128
MLA_ROPE = 64
MLA_V = 128
MLA_QK_PAD = 256
ATTN_SCALE = (MLA_NOPE + MLA_ROPE) ** -0.5
Q_SCALE = ATTN_SCALE * math.log2(math.e)
ATTN_TQ = 256
ATTN_TK = 512

RWKV_HEADS = 8
RWKV_HEAD = 64
RWKV_WIDTH = RWKV_HEADS * RWKV_HEAD
RWKV_LORA_W = 32
RWKV_LORA_A = 32
RWKV_LORA_G = 96
RWKV_GN_EPS = 64e-5
RWKV_CHUNK = 64

GDN_HEADS = 4
GDN_HEAD = 128
GDN_WIDTH = GDN_HEADS * GDN_HEAD
GDN_CONV = 5
GDN_CHUNK = 128

COL_RKV = 0
COL_GQKV = 1536
COL_Z = 3072
COL_CQ = 3584
COL_CKV = 4096
COL_LORA = 4608
COL_KR = 4864
COL_GAB = 4992
IN_COLS_PAD = 5120
HALO = 8
VMEM_LIMIT = 56 * 1024 * 1024


def _dot(a, b):
    return jnp.dot(a, b, preferred_element_type=F32)


def _dot_nt(a, b):
    return lax.dot_general(a, b, (((1,), (1,)), ((), ())), preferred_element_type=F32)


def _dot_tn(a, b):
    return lax.dot_general(a, b, (((0,), (0,)), ((), ())), preferred_element_type=F32)


def _seg_sum(x, seg):
    hi = x.astype(BF16)
    lo = (x - hi.astype(F32)).astype(BF16)
    return _dot(hi, seg) + _dot(lo, seg)


def _layer_norm(z, g, b):
    mu = jnp.mean(z, axis=-1, keepdims=True)
    zc = z - mu
    var = jnp.mean(zc * zc, axis=-1, keepdims=True)
    return zc * lax.rsqrt(var + 1e-5) * g + b


def _params(*sem):
    return pltpu.CompilerParams(dimension_semantics=sem, vmem_limit_bytes=VMEM_LIMIT)


def _mod_kernel(c_ref, w_ref, b_ref, o_ref):
    cv = c_ref[...]
    s = cv * jax.nn.sigmoid(cv)
    o_ref[0] = _dot(s.astype(BF16), w_ref[0].astype(BF16)) + b_ref[0]


def _modulation_all(cvec, w_mod, b_mod):
    L, D, N = w_mod.shape
    tn = 1024 if N % 1024 == 0 else 512
    return pl.pallas_call(
        _mod_kernel,
        grid=(L, N // tn),
        in_specs=[pl.BlockSpec((8, D), lambda l, j: (0, 0)),
                  pl.BlockSpec((1, D, tn), lambda l, j: (l, 0, j)),
                  pl.BlockSpec((1, 1, tn), lambda l, j: (l, 0, j))],
        out_specs=pl.BlockSpec((1, 8, tn), lambda l, j: (l, 0, j)),
        out_shape=jax.ShapeDtypeStruct((L, 8, N), F32),
        compiler_params=_params("parallel", "parallel"),
        name="adaln_modulation",
    )(cvec, w_mod, b_mod.reshape(L, 1, N))


def _inproj_kernel(x_ref, mod_ref, w_ref, o_ref):
    m = mod_ref[0]
    xm = x_ref[...] * (1.0 + m[1:2]) + m[0:1]
    o_ref[...] = _dot(xm.astype(BF16), w_ref[...])


def _in_proj(h, mod, w, n_lat, t_len, n_batch, tm):
    ntot, D = h.shape
    ncol = w.shape[1]
    tn = 1024

    def mod_row(i):
        return jnp.where(i * tm < n_lat, (i * tm) // t_len, n_batch)

    return pl.pallas_call(
        _inproj_kernel,
        grid=(ncol // tn, ntot // tm),
        in_specs=[pl.BlockSpec((tm, D), lambda j, i: (i, 0)),
                  pl.BlockSpec((1, 8, D), lambda j, i: (mod_row(i), 0, 0)),
                  pl.BlockSpec((D, tn), lambda j, i: (0, j))],
        out_specs=pl.BlockSpec((tm, tn), lambda j, i: (i, j)),
        out_shape=jax.ShapeDtypeStruct((ntot, ncol), F32),
        compiler_params=_params("parallel", "parallel"),
        name="in_proj",
    )(h, mod, w)


def _mla_proj_kernel(cq_ref, ckv_ref, kr_ref, qn_ref, kn_ref, wq_ref, wkv_ref, cs1_ref, cs2_ref,
                     q_ref, k_ref, v_ref):
    cs1 = cs1_ref[...]
    cs2 = cs2_ref[...]

    def rms(x, g):
        return (x * lax.rsqrt(jnp.mean(x * x, axis=-1, keepdims=True) + 1e-6) * g).astype(BF16)

    def rope(g):
        return g * cs1 + pltpu.roll(g, MLA_ROPE, 1) * cs2

    oq = _dot(rms(cq_ref[...], qn_ref[...]), wq_ref[...])
    okv = _dot(rms(ckv_ref[...], kn_ref[...]), wkv_ref[...])
    krot = rope(kr_ref[...])
    for hd in range(MLA_HEADS):
        c0 = hd * MLA_QK_PAD
        qn = oq[:, c0:c0 + MLA_NOPE]
        qr = rope(oq[:, c0 + MLA_NOPE:c0 + MLA_QK_PAD])
        q_ref[hd] = (jnp.concatenate([qn, qr], axis=1) * Q_SCALE).astype(BF16)
        k_ref[hd] = jnp.concatenate([okv[:, c0:c0 + MLA_NOPE], krot], axis=1).astype(BF16)
        v_ref[hd] = okv[:, c0 + MLA_NOPE:c0 + MLA_QK_PAD].astype(BF16)


def _mla_proj(p, q_norm, kv_norm, wq, wkv, cs1, cs2, n_lat, t_len, tm):
    ntot = p.shape[0]
    H = MLA_HEADS
    npos = t_len // tm

    def pos(i):
        return jnp.where(i * tm < n_lat, i % npos, npos)

    return pl.pallas_call(
        _mla_proj_kernel,
        grid=(ntot // tm,),
        in_specs=[pl.BlockSpec((tm, MLA_RANK), lambda i: (i, COL_CQ // MLA_RANK)),
                  pl.BlockSpec((tm, MLA_RANK), lambda i: (i, COL_CKV // MLA_RANK)),
                  pl.BlockSpec((tm, LANES), lambda i: (i, COL_KR // LANES)),
                  pl.BlockSpec((1, MLA_RANK), lambda i: (0, 0)),
                  pl.BlockSpec((1, MLA_RANK), lambda i: (0, 0)),
                  pl.BlockSpec(wq.shape, lambda i: (0, 0)),
                  pl.BlockSpec(wkv.shape, lambda i: (0, 0)),
                  pl.BlockSpec((tm, LANES), lambda i: (pos(i), 0)),
                  pl.BlockSpec((tm, LANES), lambda i: (pos(i), 0))],
        out_specs=[pl.BlockSpec((H, tm, MLA_QK_PAD), lambda i: (0, i, 0)),
                   pl.BlockSpec((H, tm, MLA_QK_PAD), lambda i: (0, i, 0)),
                   pl.BlockSpec((H, tm, MLA_V), lambda i: (0, i, 0))],
        out_shape=[jax.ShapeDtypeStruct((H, ntot, MLA_QK_PAD), BF16),
                   jax.ShapeDtypeStruct((H, ntot, MLA_QK_PAD), BF16),
                   jax.ShapeDtypeStruct((H, ntot, MLA_V), BF16)],
        compiler_params=_params("parallel"),
        name="mla_proj",
    )(p, p, p, q_norm, kv_norm, wq, wkv, cs1, cs2)


def _softmax_pv(scores, values):
    def tiles(x):
        return [x[:, i * LANES:(i + 1) * LANES] for i in range(x.shape[1] // LANES)]

    m = jnp.max(functools.reduce(jnp.maximum, [t for s in scores for t in tiles(s)]), axis=-1, keepdims=True)
    ps = [jnp.exp2(s - m) for s in scores]
    l = jnp.sum(functools.reduce(jnp.add, [t for p in ps for t in tiles(p)]), axis=-1, keepdims=True)
    acc = functools.reduce(jnp.add, [_dot(p.astype(BF16), v) for p, v in zip(ps, values)])
    return acc / l


def _attn_kernel(q_ref, kl_ref, kc_ref, vl_ref, vc_ref, o_ref, *, nq_lat, n_chunks):
    qi = pl.program_id(2)

    @pl.when(qi < nq_lat)
    def _():
        q = q_ref[0]
        scores = [_dot_nt(q, kc_ref[0])]
        values = [vc_ref[0]]
        for c in range(n_chunks):
            scores.append(_dot_nt(q, kl_ref[0, c * ATTN_TK:(c + 1) * ATTN_TK, :]))
            values.append(vl_ref[0, c * ATTN_TK:(c + 1) * ATTN_TK, :])
        o_ref[...] = _softmax_pv(scores, values).astype(o_ref.dtype)

    @pl.when(qi >= nq_lat)
    def _():
        o_ref[...] = _softmax_pv([_dot_nt(q_ref[0], kc_ref[0])], [vc_ref[0]]).astype(o_ref.dtype)


def _attention(qh, kh, vh, n_batch, t_len, ctx_len, with_ctx):
    H, ntot, _ = qh.shape
    n_lat = n_batch * t_len
    tq = ATTN_TQ
    assert ctx_len == tq and t_len % ATTN_TK == 0
    nq_lat = t_len // tq
    nq = nq_lat + (1 if with_ctx else 0)
    ctx_blk0 = n_lat // ctx_len

    def qrow(b, qi):
        return jnp.where(qi < nq_lat, b * nq_lat + qi, n_lat // tq + b)

    kern = functools.partial(_attn_kernel, nq_lat=nq_lat, n_chunks=t_len // ATTN_TK)
    return pl.pallas_call(
        kern,
        grid=(n_batch, H, nq),
        in_specs=[pl.BlockSpec((1, tq, MLA_QK_PAD), lambda b, h, qi: (h, qrow(b, qi), 0)),
                  pl.BlockSpec((1, t_len, MLA_QK_PAD), lambda b, h, qi: (h, b, 0)),
                  pl.BlockSpec((1, ctx_len, MLA_QK_PAD), lambda b, h, qi: (h, ctx_blk0 + b, 0)),
                  pl.BlockSpec((1, t_len, MLA_V), lambda b, h, qi: (h, b, 0)),
                  pl.BlockSpec((1, ctx_len, MLA_V), lambda b, h, qi: (h, ctx_blk0 + b, 0))],
        out_specs=pl.BlockSpec((tq, MLA_V), lambda b, h, qi: (qrow(b, qi), h)),
        out_shape=jax.ShapeDtypeStruct((ntot, H * MLA_V), BF16),
        compiler_params=_params("parallel", "parallel", "arbitrary"),
        name="mla_attention",
    )(qh, kh, kh, vh, vh)


def _tri(n, reverse, transpose=False):
    i = lax.broadcasted_iota(jnp.int32, (n, n), 0)
    j = lax.broadcasted_iota(jnp.int32, (n, n), 1)
    if transpose:
        i, j = j, i
    return ((j >= i) if reverse else (j <= i)).astype(F32)


def _unit_tri_inverses(nmats, revs, size):
    n = nmats[0].shape[0]
    sub = 8
    ii = lax.broadcasted_iota(jnp.int32, (n, n), 0)
    jj = lax.broadcasted_iota(jnp.int32, (n, n), 1)
    eye = (ii == jj).astype(F32)
    diag = (ii // sub) == (jj // sub)
    jloc = jj % sub
    nds = [jnp.where(diag, nm, 0.0) for nm in nmats]
    xs = [eye for _ in nmats]
    for t in range(sub - 1):
        sel = {rev: jloc == (sub - 1 - t if rev else t) for rev in set(revs)}
        cols = [jnp.sum(jnp.where(sel[rev], nd, 0.0), axis=1, keepdims=True) for nd, rev in zip(nds, revs)]
        rows = []
        for x, rev in zip(xs, revs):
            j = sub - 1 - t if rev else t
            rows.append(jnp.broadcast_to(x.reshape(n // sub, sub, n)[:, j:j + 1, :], (n // sub, sub, n)).reshape(n, n))
        xs = [x + c * r for x, c, r in zip(xs, cols, rows)]
    s = sub
    while s < size:
        off = ((ii // (2 * s)) == (jj // (2 * s))) & ((ii // s) != (jj // s))
        xbs = [x.astype(BF16) for x in xs]
        offs = [jnp.where(off, nm, 0.0).astype(BF16) for nm in nmats]
        t1 = [_dot(xb, o).astype(BF16) for xb, o in zip(xbs, offs)]
        xs = [x + _dot(t, xb) for x, t, xb in zip(xs, t1, xbs)]
        s *= 2
    return xs


def _rwkv_chunks(chains, states, revs):
    C = RWKV_CHUNK
    n2 = 2 * C
    nc = len(revs)
    rs, vs, as_, lws, kds, bds = ([c[i] for c in chains] for i in range(6))
    tri = {rev: _tri(C, rev) for rev in set(revs)}
    lane = lax.broadcasted_iota(jnp.int32, (C, LANES), 1)
    m0 = (lane < RWKV_HEAD).astype(F32)
    m1 = 1.0 - m0
    ii = lax.broadcasted_iota(jnp.int32, (n2, n2), 0)
    jj = lax.broadcasted_iota(jnp.int32, (n2, n2), 1)
    same = (ii // C) == (jj // C)
    il, jl = ii % C, jj % C
    strict = {rev: same & ((jl > il) if rev else (jl < il)) for rev in set(revs)}
    incl = {rev: same & ((jl >= il) if rev else (jl <= il)) for rev in set(revs)}

    def ex(x):
        return jnp.concatenate([x * m0, x * m1], axis=0)

    cums = [jnp.dot(tri[rev], lw, precision=HIGHEST, preferred_element_type=F32) for lw, rev in zip(lws, revs)]
    tots = [jnp.sum(lw, axis=0, keepdims=True) for lw in lws]
    e_in = [jnp.exp(cum) for cum in cums]
    e_ex = [jnp.exp(cum - lw) for cum, lw in zip(cums, lws)]
    e_neg = [jnp.exp(-cum) for cum in cums]
    e_rem = [jnp.exp(tot - cum) for tot, cum in zip(tots, cums)]
    ats = [ex(a * e) for a, e in zip(as_, e_ex)]
    rts = [ex(r * e) for r, e in zip(rs, e_in)]
    bts = [ex(b * e) for b, e in zip(bds, e_neg)]
    kts = [ex(k * e) for k, e in zip(kds, e_neg)]
    bhs = [ex(b * e).astype(BF16) for b, e in zip(bds, e_rem)]
    khs = [ex(k * e).astype(BF16) for k, e in zip(kds, e_rem)]
    ves = [ex(v).astype(BF16) for v in vs]

    aas = [_dot_nt(jnp.concatenate([at, rt], axis=0).astype(BF16), jnp.concatenate([bt, kt], axis=0).astype(BF16))
           for at, rt, bt, kt in zip(ats, rts, bts, kts)]
    a_ab = [jnp.where(strict[rev], aa[:n2, :n2], 0.0) for aa, rev in zip(aas, revs)]
    a_ak = [jnp.where(strict[rev], aa[:n2, n2:], 0.0).astype(BF16) for aa, rev in zip(aas, revs)]
    a_rb = [jnp.where(incl[rev], aa[n2:, :n2], 0.0).astype(BF16) for aa, rev in zip(aas, revs)]
    a_rk = [jnp.where(incl[rev], aa[n2:, n2:], 0.0).astype(BF16) for aa, rev in zip(aas, revs)]

    akv = [_dot(m, ve) for m, ve in zip(a_ak, ves)]
    tinv = _unit_tri_inverses(a_ab, revs, C)
    zbs = [_dot(ti.astype(BF16), jnp.concatenate([at, kv], axis=1).astype(BF16)).astype(BF16)
           for ti, at, kv in zip(tinv, ats, akv)]
    w2 = [_dot(m, zb) for m, zb in zip(a_rb, zbs)]
    rhat = [(rt + w[:, :LANES]).astype(BF16) for rt, w in zip(rts, w2)]
    y0 = [w[:, LANES:] + _dot(m, ve) for w, m, ve in zip(w2, a_rk, ves)]
    gmt = [_dot_tn(zb[:, :LANES], bh).astype(BF16) for zb, bh in zip(zbs, bhs)]
    s0t = [_dot_tn(jnp.concatenate([zb[:, LANES:], ve], axis=0), jnp.concatenate([bh, kh], axis=0))
           for zb, ve, bh, kh in zip(zbs, ves, bhs, khs)]
    sbs = [s.astype(BF16) for s in states]
    yes = [_dot_nt(rh, sb) + y for rh, sb, y in zip(rhat, sbs, y0)]
    ys = [ye[:C] + ye[C:] for ye in yes]
    new_states = [s * jnp.exp(tot) + _dot(sb, g) + s0 for s, tot, sb, g, s0 in zip(states, tots, sbs, gmt, s0t)]
    return ys, new_states


def _rwkv_scan_kernel(rf, rb, vf, vb, af, ab, lw0, lw1, kd0, kd1, b0, b1, yf_ref, yb_ref, s_ref):
    @pl.when(pl.program_id(1) == 0)
    def _():
        s_ref[...] = jnp.zeros_like(s_ref)

    dirs = ((rf, vf, af, lw0, kd0, b0, yf_ref), (rb, vb, ab, lw1, kd1, b1, yb_ref))
    npair = RWKV_WIDTH // LANES
    chains, states, revs, outs = [], [], [], []
    for d, (r_, v_, a_, lw_, kd_, b_, y_) in enumerate(dirs):
        for j in range(npair):
            sl = slice(j * LANES, (j + 1) * LANES)
            chains.append((r_[:, sl], v_[:, sl], a_[:, sl], lw_[:, sl], kd_[:, sl], b_[:, sl]))
            states.append(s_ref[d, j])
            revs.append(d == 1)
            outs.append((y_, sl, d, j))
    ys, new_states = _rwkv_chunks(chains, states, revs)
    for (y_, sl, d, j), y, s_new in zip(outs, ys, new_states):
        s_ref[d, j] = s_new
        y_[:, sl] = y


def _chunk_maps(n_batch, t_len, ctx_len, chunk):
    nctx, nlat = ctx_len // chunk, t_len // chunk
    base = n_batch * nlat

    def fwd(b, i):
        return jnp.where(i < nctx, base + b * nctx + i, b * nlat + i - nctx)

    def bwd(b, i):
        return jnp.where(i < nctx, base + b * nctx + (nctx - 1 - i), b * nlat + (nlat - 1 - (i - nctx)))

    return fwd, bwd, nctx + nlat


def _rwkv_scan(r, v, a, lw0, lw1, kd0, kd1, b0, b1, n_batch, t_len, ctx_len):
    ntot, W = r.shape
    C = RWKV_CHUNK
    fwd, bwd, nsteps = _chunk_maps(n_batch, t_len, ctx_len, C)
    fs = pl.BlockSpec((C, W), lambda b, i: (fwd(b, i), 0))
    bs = pl.BlockSpec((C, W), lambda b, i: (bwd(b, i), 0))
    return pl.pallas_call(
        _rwkv_scan_kernel,
        grid=(n_batch, nsteps),
        in_specs=[fs, bs, fs, bs, fs, bs, fs, bs, fs, bs, fs, bs],
        out_specs=[fs, bs],
        out_shape=[jax.ShapeDtypeStruct((ntot, W), F32)] * 2,
        scratch_shapes=[pltpu.VMEM((2, W // LANES, LANES, LANES), F32)],
        compiler_params=_params("parallel", "arbitrary"),
        name="rwkv7_scan",
    )(r, r, v, v, a, a, lw0, lw1, kd0, kd1, b0, b1)


def _gdn_scan_kernel(qf, qb, kf, kb, vf, vb, gf, gb, of_ref, ob_ref, s_ref):
    @pl.when(pl.program_id(1) == 0)
    def _():
        s_ref[...] = jnp.zeros_like(s_ref)

    C = GDN_CHUNK
    DH = GDN_HEAD
    ii = lax.broadcasted_iota(jnp.int32, (C, C), 0)
    jj = lax.broadcasted_iota(jnp.int32, (C, C), 1)
    strict = {False: jj < ii, True: jj > ii}
    incl = {False: jj <= ii, True: jj >= ii}
    qs, ks, vs, gcs, grs, gts, betas, revs, outs = [], [], [], [], [], [], [], [], []
    for d, (q_, k_, v_, g_, o_) in enumerate(((qf, kf, vf, gf, of_ref), (qb, kb, vb, gb, ob_ref))):
        rev = d == 1
        gbv = g_[...]
        gcol = jnp.dot(_tri(C, rev), gbv, precision=HIGHEST, preferred_element_type=F32)
        grow = jnp.dot(gbv.T, _tri(C, rev, transpose=True), precision=HIGHEST, preferred_element_type=F32)
        gtot = jnp.sum(gbv, axis=0, keepdims=True)
        for hd in range(GDN_HEADS):
            sl = slice(hd * DH, (hd + 1) * DH)
            cg = d * GDN_HEADS + hd
            cb = 2 * GDN_HEADS + cg
            qs.append(q_[:, sl]); ks.append(k_[:, sl]); vs.append(v_[:, sl])
            gcs.append(gcol[:, cg:cg + 1]); grs.append(grow[cg:cg + 1, :]); gts.append(gtot[:, cg:cg + 1])
            betas.append(gbv[:, cb:cb + 1]); revs.append(rev); outs.append((o_, sl, d, hd))

    decs = [jnp.exp(jnp.where(incl[rev], gc - gr, -1e30)) for gc, gr, rev in zip(gcs, grs, revs)]
    kbetas = [k * b for k, b in zip(ks, betas)]
    qks = [_dot_nt(jnp.concatenate([kb_, q], axis=0).astype(BF16), k.astype(BF16)) for kb_, q, k in zip(kbetas, qs, ks)]
    nmats = [jnp.where(strict[rev], -qk[:C] * dec, 0.0) for qk, dec, rev in zip(qks, decs, revs)]
    aints = [jnp.where(incl[rev], qk[C:] * dec, 0.0).astype(BF16) for qk, dec, rev in zip(qks, decs, revs)]
    egs = [jnp.exp(gc) for gc in gcs]
    rhss = [jnp.concatenate([v * b, kb_ * eg], axis=1).astype(BF16) for v, b, kb_, eg in zip(vs, betas, kbetas, egs)]
    khats = [(k * jnp.exp(gt - gc)).astype(BF16) for k, gt, gc in zip(ks, gts, gcs)]
    tinvs = _unit_tri_inverses(nmats, revs, C)
    sols = [_dot(ti.astype(BF16), rhs).astype(BF16) for ti, rhs in zip(tinvs, rhss)]
    qos = [_dot(ai, sol) for ai, sol in zip(aints, sols)]
    qhats = [(q * eg - qo[:, DH:]).astype(BF16) for q, eg, qo in zip(qs, egs, qos)]
    gss = [_dot_tn(kh, sol) for kh, sol in zip(khats, sols)]
    states = [s_ref[d, hd] for (_, _, d, hd) in outs]
    sbs = [s.astype(BF16) for s in states]
    os_ = [_dot(qh, sb) + qo[:, :DH] for qh, sb, qo in zip(qhats, sbs, qos)]
    news = [s * jnp.exp(gt) - _dot(gs[:, DH:].astype(BF16), sb) + gs[:, :DH]
            for s, gt, gs, sb in zip(states, gts, gss, sbs)]
    for (o_, sl, d, hd), o, s_new in zip(outs, os_, news):
        o_[:, sl] = o
        s_ref[d, hd] = s_new


def _gdn_scan(q, k, v, gb, n_batch, t_len, ctx_len):
    ntot, W = q.shape
    C = GDN_CHUNK
    fwd, bwd, nsteps = _chunk_maps(n_batch, t_len, ctx_len, C)
    fs = pl.BlockSpec((C, W), lambda b, i: (fwd(b, i), 0))
    bs = pl.BlockSpec((C, W), lambda b, i: (bwd(b, i), 0))
    gfs = pl.BlockSpec((C, LANES), lambda b, i: (fwd(b, i), 0))
    gbs = pl.BlockSpec((C, LANES), lambda b, i: (bwd(b, i), 0))
    return pl.pallas_call(
        _gdn_scan_kernel,
        grid=(n_batch, nsteps),
        in_specs=[fs, bs, fs, bs, fs, bs, gfs, gbs],
        out_specs=[fs, bs],
        out_shape=[jax.ShapeDtypeStruct((ntot, W), F32)] * 2,
        scratch_shapes=[pltpu.VMEM((2, GDN_HEADS, GDN_HEAD, GDN_HEAD), F32)],
        compiler_params=_params("parallel", "arbitrary"),
        name="gdn_scan",
    )(q, q, k, k, v, v, gb, gb)


def _outproj_kernel(x_ref, a_ref, yf_ref, yb_ref, r_ref, v_ref, kd0_ref, kd1_ref, g_ref, of_ref, ob_ref, z_ref,
                    rk_ref, gng_ref, gnb_ref, gdn_ref, seg_ref, w_ref, mod_ref, lng_ref, lnb_ref, o_ref, *, alpha):
    seg = seg_ref[...]
    y = yf_ref[...] + yb_ref[...]
    yc = y - _seg_sum(y, seg) * (1.0 / RWKV_HEAD)
    var = _seg_sum(yc * yc, seg) * (1.0 / RWKV_HEAD)
    yn = yc * lax.rsqrt(var + RWKV_GN_EPS) * gng_ref[...] + gnb_ref[...]
    bonus = _seg_sum(r_ref[...] * (kd0_ref[...] + kd1_ref[...]) * rk_ref[...], seg) * v_ref[...]
    b_mix = ((yn + bonus) * g_ref[...]).astype(BF16)
    o = of_ref[...] + ob_ref[...]
    z = z_ref[...]
    parts = []
    for hd in range(GDN_HEADS):
        oh = o[:, hd * GDN_HEAD:(hd + 1) * GDN_HEAD]
        parts.append(oh * lax.rsqrt(jnp.mean(oh * oh, axis=-1, keepdims=True) + 1e-6))
    c_mix = (jnp.concatenate(parts, axis=1) * gdn_ref[...] * (z * jax.nn.sigmoid(z))).astype(BF16)

    na, nb = a_ref.shape[1], RWKV_WIDTH
    acc = _dot(a_ref[...], w_ref[0:na, :])
    acc += _dot(b_mix, w_ref[na:na + nb, :])
    acc += _dot(c_mix, w_ref[na + nb:, :])
    gate = mod_ref[0][2:3]
    o_ref[...] = _layer_norm(alpha * x_ref[...] + gate * acc, lng_ref[...], lnb_ref[...])


def _out_proj(h, a, rw, gd, p, r_k, gn_g, gn_b, gdn_norm, seg, w, mod, ln_g, ln_b,
              n_rows, n_lat, t_len, n_batch, tm, alpha):
    D = h.shape[1]
    W = RWKV_WIDTH

    def mod_row(i):
        return jnp.where(i * tm < n_lat, (i * tm) // t_len, n_batch)

    tok = pl.BlockSpec((tm, W), lambda i: (i, 0))
    vec = pl.BlockSpec((1, W), lambda i: (0, 0))
    return pl.pallas_call(
        functools.partial(_outproj_kernel, alpha=alpha),
        grid=(n_rows // tm,),
        in_specs=[pl.BlockSpec((tm, D), lambda i: (i, 0)),
                  pl.BlockSpec((tm, a.shape[1]), lambda i: (i, 0))]
                 + [tok] * 9
                 + [pl.BlockSpec((tm, W), lambda i: (i, COL_Z // W)),
                    vec, vec, vec, vec,
                    pl.BlockSpec(seg.shape, lambda i: (0, 0)),
                    pl.BlockSpec(w.shape, lambda i: (0, 0)),
                    pl.BlockSpec((1, 8, D), lambda i: (mod_row(i), 0, 0)),
                    pl.BlockSpec((1, D), lambda i: (0, 0)),
                    pl.BlockSpec((1, D), lambda i: (0, 0))],
        out_specs=pl.BlockSpec((tm, D), lambda i: (i, 0)),
        out_shape=jax.ShapeDtypeStruct((n_rows, D), F32),
        compiler_params=_params("parallel"),
        name="out_proj_ln",
    )(h, a, *rw, *gd, p, r_k, gn_g, gn_b, gdn_norm, seg, w, mod, ln_g, ln_b)


def _ffn_kernel(x_ref, mod_ref, wg_ref, wu_ref, wd_ref, g_ref, beta_ref, o_ref, xm_ref, acc_ref, *, alpha):
    f = pl.program_id(1)

    @pl.when(f == 0)
    def _():
        m = mod_ref[0]
        xm_ref[...] = (x_ref[...] * (1.0 + m[4:5]) + m[3:4]).astype(BF16)
        acc_ref[...] = jnp.zeros_like(acc_ref)

    xm = xm_ref[...]
    gate = _dot(xm, wg_ref[...])
    up = _dot(xm, wu_ref[...])
    act = gate * jax.nn.sigmoid(gate) * up
    acc_ref[...] += _dot(act.astype(BF16), wd_ref[...])

    @pl.when(f == pl.num_programs(1) - 1)
    def _():
        m = mod_ref[0]
        o_ref[...] = _layer_norm(alpha * x_ref[...] + m[5:6] * acc_ref[...], g_ref[...], beta_ref[...])


def _ffn(h, mod, wg, wu, wd, ln_g, ln_b, n_rows, n_lat, t_len, n_batch, tm, tf, alpha):
    D = h.shape[1]
    F = wg.shape[1]

    def mod_row(i):
        return jnp.where(i * tm < n_lat, (i * tm) // t_len, n_batch)

    return pl.pallas_call(
        functools.partial(_ffn_kernel, alpha=alpha),
        grid=(n_rows // tm, F // tf),
        in_specs=[pl.BlockSpec((tm, D), lambda i, f: (i, 0)),
                  pl.BlockSpec((1, 8, D), lambda i, f: (mod_row(i), 0, 0)),
                  pl.BlockSpec((D, tf), lambda i, f: (0, f)),
                  pl.BlockSpec((D, tf), lambda i, f: (0, f)),
                  pl.BlockSpec((tf, D), lambda i, f: (f, 0)),
                  pl.BlockSpec((1, D), lambda i, f: (0, 0)),
                  pl.BlockSpec((1, D), lambda i, f: (0, 0))],
        out_specs=pl.BlockSpec((tm, D), lambda i, f: (i, 0)),
        out_shape=jax.ShapeDtypeStruct((n_rows, D), F32),
        scratch_shapes=[pltpu.VMEM((tm, D), BF16), pltpu.VMEM((tm, D), F32)],
        compiler_params=_params("parallel", "arbitrary"),
        name="ffn_ln",
    )(h, mod, wg, wu, wd, ln_g, ln_b)


def _in_proj_columns():
    kr0 = 2 * MLA_RANK
    swap = np.concatenate([np.arange(16, 32), np.arange(0, 16), np.arange(48, 64), np.arange(32, 48)])
    pr0 = kr0 + MLA_ROPE
    n_lora = 2 * RWKV_LORA_W + 2 * RWKV_LORA_A + RWKV_LORA_G
    pg0 = pr0 + 3 * RWKV_WIDTH + n_lora
    nh4 = 4 * GDN_HEADS
    idx = -np.ones(IN_COLS_PAD, np.int64)

    def put(dst, src):
        idx[dst:dst + len(src)] = src

    put(COL_RKV, pr0 + np.arange(3 * RWKV_WIDTH))
    put(COL_GQKV, pg0 + np.arange(3 * GDN_WIDTH))
    put(COL_Z, pg0 + 3 * GDN_WIDTH + np.arange(GDN_WIDTH))
    put(COL_CQ, np.arange(MLA_RANK))
    put(COL_CKV, MLA_RANK + np.arange(MLA_RANK))
    put(COL_LORA, pr0 + 3 * RWKV_WIDTH + np.arange(n_lora))
    put(COL_KR, kr0 + np.arange(MLA_ROPE))
    put(COL_KR + MLA_ROPE, kr0 + swap)
    put(COL_GAB, pg0 + 4 * GDN_WIDTH + np.arange(nh4))
    return idx


def _take_cols(w, idx):
    safe = jnp.asarray(np.maximum(idx, 0), jnp.int32)
    mask = jnp.asarray(idx >= 0)
    return jnp.where(mask, jnp.take(w, safe, axis=-1), 0.0)


def _uq_columns():
    swap = np.concatenate([np.arange(16, 32), np.arange(0, 16), np.arange(48, 64), np.arange(32, 48)])
    per = MLA_NOPE + MLA_ROPE
    out = []
    for hd in range(MLA_HEADS):
        out += [hd * per + np.arange(MLA_NOPE), hd * per + MLA_NOPE + np.arange(MLA_ROPE), hd * per + MLA_NOPE + swap]
    return np.concatenate(out)


def _rope_tables(t_len, tm):
    rows = t_len // GRID_W
    row = jnp.repeat(jnp.arange(rows), GRID_W)
    col = jnp.tile(jnp.arange(GRID_W), rows)
    n_freq = MLA_ROPE // 4
    inv = ROPE_THETA ** (-jnp.arange(n_freq, dtype=F32) / n_freq)
    ar = row.astype(F32)[:, None] * inv
    ac = col.astype(F32)[:, None] * inv
    cos = jnp.concatenate([jnp.cos(ar), jnp.cos(ar), jnp.cos(ac), jnp.cos(ac)], axis=1)
    sin = jnp.concatenate([-jnp.sin(ar), jnp.sin(ar), -jnp.sin(ac), jnp.sin(ac)], axis=1)
    zer = jnp.zeros((t_len, LANES - MLA_ROPE), F32)
    cs1 = jnp.concatenate([cos, zer], axis=1)
    cs2 = jnp.concatenate([sin, zer], axis=1)
    ident = jnp.concatenate([jnp.ones((tm, MLA_ROPE), F32), jnp.zeros((tm, LANES - MLA_ROPE), F32)], axis=1)
    return jnp.concatenate([cs1, ident], axis=0), jnp.concatenate([cs2, jnp.zeros((tm, LANES), F32)], axis=0)


def _fill_halo(ext_ref, cur_ref, prev_ref, next_ref, first, last):
    tm = cur_ref.shape[0]
    ext_ref[0:HALO, :] = jnp.where(first, 0.0, prev_ref[...])
    ext_ref[HALO:HALO + tm, :] = cur_ref[...]
    ext_ref[HALO + tm:, :] = jnp.where(last, 0.0, next_ref[...])


def _seq_edges(tm, n_lat, t_len, ctx_len):
    t0 = pl.program_id(0) * tm
    lat = t0 < n_lat
    first = jnp.where(lat, t0 % t_len == 0, (t0 - n_lat) % ctx_len == 0)
    last = jnp.where(lat, (t0 + tm) % t_len == 0, (t0 - n_lat + tm) % ctx_len == 0)
    return first, last


def _softplus(x):
    return jnp.maximum(x, 0.0) + jnp.log(1.0 + jnp.exp(-jnp.abs(x)))


def _rwkv_prep_kernel(rkv_ref, rkv_p, rkv_n, lg_ref, lg_p, lg_n, mu_rkv, mu_lg, w0_ref, a0_ref, w2_ref, a2_ref,
                      g2_ref, kk_ref, ka_ref, seg_ref,
                      r_o, v_o, a_o, lw0_o, lw1_o, kd0_o, kd1_o, b0_o, b1_o, g_o, ext_rkv, ext_lg,
                      *, n_lat, t_len, ctx_len):
    tm = rkv_ref.shape[0]
    W = RWKV_WIDTH
    first, last = _seq_edges(tm, n_lat, t_len, ctx_len)
    _fill_halo(ext_rkv, rkv_ref, rkv_p, rkv_n, first, last)
    _fill_halo(ext_lg, lg_ref, lg_p, lg_n, first, last)

    def mixed(ext, mu):
        cur = ext[HALO:HALO + tm, :]
        sh = 0.5 * (ext[HALO - 1:HALO - 1 + tm, :] + ext[HALO + 1:HALO + 1 + tm, :])
        return cur + (sh - cur) * mu[...]

    p = mixed(ext_rkv, mu_rkv)
    lg = mixed(ext_lg, mu_lg)
    r, k, v = p[:, :W], p[:, W:2 * W], p[:, 2 * W:]
    lora = lg[:, :LANES]
    tw = jnp.tanh(lora).astype(BF16)
    ab = lora.astype(BF16)
    kkr = k * kk_ref[...]
    kk = kkr * lax.rsqrt(_seg_sum(kkr * kkr, seg_ref[...]) + 1e-12)
    r_o[...] = r
    v_o[...] = v
    a_o[...] = -kk
    g_o[...] = _dot(jax.nn.sigmoid(lg[:, LANES:]).astype(BF16), g2_ref[...])
    for d, (lw_o, kd_o, b_o) in enumerate(((lw0_o, kd0_o, b0_o), (lw1_o, kd1_o, b1_o))):
        w_log = -_softplus(-(w0_ref[d:d + 1, :] + _dot(tw, w2_ref[d]))) - 0.5
        lw_o[...] = -jnp.exp(w_log)
        a_lr = jax.nn.sigmoid(a0_ref[d:d + 1, :] + _dot(ab, a2_ref[d]))
        kd_o[...] = k * (1.0 + (a_lr - 1.0) * ka_ref[...])
        b_o[...] = kk * a_lr


def _halo_specs(tm, width, col, ntot):
    nb = tm // HALO
    last_blk = ntot // HALO - 1
    return [pl.BlockSpec((tm, width), lambda i: (i, col)),
            pl.BlockSpec((HALO, width), lambda i: (jnp.maximum(i * nb - 1, 0), col)),
            pl.BlockSpec((HALO, width), lambda i: (jnp.minimum((i + 1) * nb, last_blk), col))]


def _rwkv_prep(p, mu_rkv, mu_lg, w0, a0, w2p, a2p, g2p, k_k, k_a, seg, n_lat, t_len, ctx_len, tm):
    ntot = p.shape[0]
    W = RWKV_WIDTH
    full = lambda arr: pl.BlockSpec(arr.shape, lambda i: (0,) * arr.ndim)
    params = (mu_rkv, mu_lg, w0, a0, w2p, a2p, g2p, k_k, k_a, seg)
    return pl.pallas_call(
        functools.partial(_rwkv_prep_kernel, n_lat=n_lat, t_len=t_len, ctx_len=ctx_len),
        grid=(ntot // tm,),
        in_specs=_halo_specs(tm, 3 * W, COL_RKV // (3 * W), ntot) + _halo_specs(tm, 2 * LANES, COL_LORA // (2 * LANES), ntot)
                 + [full(a) for a in params],
        out_specs=[pl.BlockSpec((tm, W), lambda i: (i, 0))] * 10,
        out_shape=[jax.ShapeDtypeStruct((ntot, W), F32)] * 10,
        scratch_shapes=[pltpu.VMEM((tm + 2 * HALO, 3 * W), F32), pltpu.VMEM((tm + 2 * HALO, 2 * LANES), F32)],
        compiler_params=_params("parallel"),
        name="rwkv7_prep",
    )(p, p, p, p, p, p, *params)


def _gdn_prep_kernel(x_ref, x_p, x_n, ab_ref, cw_ref, alog_ref, dt_ref, q_o, k_o, v_o, gb_o, ext,
                     *, n_lat, t_len, ctx_len):
    tm = x_ref.shape[0]
    W = GDN_WIDTH
    first, last = _seq_edges(tm, n_lat, t_len, ctx_len)
    _fill_halo(ext, x_ref, x_p, x_n, first, last)
    half = GDN_CONV // 2
    conv = functools.reduce(jnp.add, [ext[HALO - half + j:HALO - half + j + tm, :] * cw_ref[j:j + 1, :]
                                      for j in range(GDN_CONV)])
    act = conv * jax.nn.sigmoid(conv)

    def l2n(x):
        parts = []
        for hd in range(GDN_HEADS):
            xh = x[:, hd * GDN_HEAD:(hd + 1) * GDN_HEAD]
            parts.append(xh * lax.rsqrt(jnp.sum(xh * xh, axis=-1, keepdims=True) + 1e-12))
        return jnp.concatenate(parts, axis=1)

    q_o[...] = l2n(act[:, :W]) * GDN_HEAD ** -0.5
    k_o[...] = l2n(act[:, W:2 * W])
    v_o[...] = act[:, 2 * W:]
    ab = ab_ref[...]
    lane = lax.broadcasted_iota(jnp.int32, ab.shape, 1)
    nh2 = 2 * GDN_HEADS
    glog = -jnp.exp(alog_ref[...]) * _softplus(ab + dt_ref[...])
    gb_o[...] = jnp.where(lane < nh2, glog, jnp.where(lane < 2 * nh2, jax.nn.sigmoid(ab), 0.0))


def _gdn_prep(p, conv_w, a_log, dt_bias, n_lat, t_len, ctx_len, tm):
    ntot = p.shape[0]
    W = GDN_WIDTH
    full = lambda arr: pl.BlockSpec(arr.shape, lambda i: (0,) * arr.ndim)
    return pl.pallas_call(
        functools.partial(_gdn_prep_kernel, n_lat=n_lat, t_len=t_len, ctx_len=ctx_len),
        grid=(ntot // tm,),
        in_specs=_halo_specs(tm, 3 * W, COL_GQKV // (3 * W), ntot)
                 + [pl.BlockSpec((tm, LANES), lambda i: (i, COL_GAB // LANES)), full(conv_w), full(a_log), full(dt_bias)],
        out_specs=[pl.BlockSpec((tm, W), lambda i: (i, 0))] * 3 + [pl.BlockSpec((tm, LANES), lambda i: (i, 0))],
        out_shape=[jax.ShapeDtypeStruct((ntot, W), F32)] * 3 + [jax.ShapeDtypeStruct((ntot, LANES), F32)],
        scratch_shapes=[pltpu.VMEM((tm + 2 * HALO, 3 * W), F32)],
        compiler_params=_params("parallel"),
        name="gdn_prep",
    )(p, p, p, p, conv_w, a_log, dt_bias)


def kernel(x, c, ctx, c_ctx, w_mod, b_mod, w_in, mla_q_norm, mla_kv_norm, mla_w_uq, mla_w_ukv, rwkv_mu, rwkv_w0, rwkv_w2, rwkv_a0, rwkv_a2, rwkv_g2, rwkv_k_k, rwkv_k_a, rwkv_r_k, rwkv_gn_g, rwkv_gn_b, gdn_conv, gdn_a_log, gdn_dt_bias, gdn_norm, w_out, ln1_g, ln1_b, ffn_w_gate, ffn_w_up, ffn_w_down, ln2_g, ln2_b):
    B, T, D = x.shape
    CTX = ctx.shape[1]
    L = w_mod.shape[0]
    n_lat, n_ctx = B * T, B * CTX
    ntot = n_lat + n_ctx
    alpha = (2.0 * L) ** 0.25
    tm = 256
    tm_ffn = 512 if (n_lat % 512 == 0 and n_ctx % 512 == 0 and T % 512 == 0) else tm
    tf = 512
    assert T % tm == 0 and n_ctx % tm == 0 and B + 1 <= 8

    cvec = jnp.concatenate([c, c_ctx[None], jnp.zeros((8 - B - 1, D), F32)], axis=0)
    mod_all = _modulation_all(cvec, w_mod, b_mod).reshape(L, 8, 6, D)[:, :B + 1]
    mod_all = jnp.pad(mod_all, ((0, 0), (0, 0), (0, 2), (0, 0)))

    in_idx = _in_proj_columns()
    w_in_p = _take_cols(w_in, in_idx).astype(BF16)
    uq_idx = _uq_columns()
    w_uq_p = jnp.take(mla_w_uq, jnp.asarray(uq_idx, jnp.int32), axis=-1).astype(BF16)
    w_ukv_b = mla_w_ukv.astype(BF16)
    w_out_b = w_out.astype(BF16)
    wg_b, wu_b, wd_b = ffn_w_gate.astype(BF16), ffn_w_up.astype(BF16), ffn_w_down.astype(BF16)
    cs1, cs2 = _rope_tables(T, tm)

    W = RWKV_WIDTH
    mu_rkv = rwkv_mu[:, None, :3 * W]
    mu_lg = jnp.pad(rwkv_mu[:, None, 3 * W:], ((0, 0), (0, 0), (0, 2 * LANES - (rwkv_mu.shape[1] - 3 * W))))
    lw, la = RWKV_LORA_W, RWKV_LORA_A
    w2p = jnp.zeros((L, 2, LANES, W), F32)
    a2p = jnp.zeros((L, 2, LANES, W), F32)
    for d in range(2):
        w2p = w2p.at[:, d, d * lw:(d + 1) * lw].set(rwkv_w2[:, d])
        a2p = a2p.at[:, d, 2 * lw + d * la:2 * lw + (d + 1) * la].set(rwkv_a2[:, d])
    w2p, a2p = w2p.astype(BF16), a2p.astype(BF16)
    g2p = jnp.pad(rwkv_g2, ((0, 0), (0, LANES - RWKV_LORA_G), (0, 0))).astype(BF16)
    lane_head = np.arange(W) // RWKV_HEAD
    seg = jnp.asarray(lane_head[:, None] == lane_head[None, :], BF16)
    conv_p = jnp.pad(gdn_conv, ((0, 0), (0, 8 - GDN_CONV), (0, 0)))
    nh2 = 2 * GDN_HEADS
    alog_p = jnp.pad(gdn_a_log.reshape(L, 1, nh2), ((0, 0), (0, 0), (0, LANES - nh2)))
    dt_p = jnp.pad(gdn_dt_bias.reshape(L, 1, nh2), ((0, 0), (0, 0), (0, LANES - nh2)))
    r_k = rwkv_r_k.reshape(L, 1, W)
    gdn_norm_t = jnp.tile(gdn_norm, (1, GDN_HEADS))[:, None, :]

    h = jnp.concatenate([x.reshape(n_lat, D), ctx.reshape(n_ctx, D)], axis=0)
    for l in range(L):
        last = l == L - 1
        mod = mod_all[l]
        p = _in_proj(h, mod, w_in_p[l], n_lat, T, B, tm)

        qh, kh, vh = _mla_proj(p, mla_q_norm[l][None], mla_kv_norm[l][None], w_uq_p[l], w_ukv_b[l], cs1, cs2,
                               n_lat, T, tm)
        a_mix = _attention(qh, kh, vh, B, T, CTX, with_ctx=not last)

        r, v, a, lw0, lw1, kd0, kd1, b0, b1, g = _rwkv_prep(
            p, mu_rkv[l], mu_lg[l], rwkv_w0[l], rwkv_a0[l], w2p[l], a2p[l], g2p[l],
            rwkv_k_k[l][None], rwkv_k_a[l][None], seg, n_lat, T, CTX, tm)
        yf, yb = _rwkv_scan(r, v, a, lw0, lw1, kd0, kd1, b0, b1, B, T, CTX)

        gq, gk, gv, gb = _gdn_prep(p, conv_p[l], alog_p[l], dt_p[l], n_lat, T, CTX, tm)
        of, ob = _gdn_scan(gq, gk, gv, gb, B, T, CTX)

        n_rows = n_lat if last else ntot
        h1 = _out_proj(h, a_mix, (yf, yb, r, v, kd0, kd1, g), (of, ob), p, r_k[l], rwkv_gn_g[l][None],
                       rwkv_gn_b[l][None], gdn_norm_t[l], seg, w_out_b[l], mod, ln1_g[l][None], ln1_b[l][None],
                       n_rows, n_lat, T, B, tm, alpha)
        h = _ffn(h1, mod, wg_b[l], wu_b[l], wd_b[l], ln2_g[l][None], ln2_b[l][None],
                 n_rows, n_lat, T, B, tm_ffn, tf, alpha)
    return h[:n_lat].reshape(B, T, D)
```

```python
import functools
import math

import numpy as np
import jax
import jax.numpy as jnp
from jax import lax
from jax.experimental import pallas as pl
from jax.experimental.pallas import tpu as pltpu

F32 = jnp.float32
BF16 = jnp.bfloat16
HIGHEST = lax.Precision.HIGHEST

LANES = 128
SUBLANES = 8
GRID_W = 64
ROPE_THETA = 10000.0

MLA_HEADS = 8
MLA_RANK = 512
MLA_NOPE = 128
MLA_ROPE = 64
MLA_V = 128
MLA_QK_PAD = 256
ATTN_SCALE = (MLA_NOPE + MLA_ROPE) ** -0.5
Q_SCALE = ATTN_SCALE * math.log2(math.e)
ATTN_TQ = 256
ATTN_TK = 512
ATTN_HEADS_PER_STEP = 2

RWKV_HEADS = 8
RWKV_HEAD = 64
RWKV_WIDTH = RWKV_HEADS * RWKV_HEAD
RWKV_LORA_W = 32
RWKV_LORA_A = 32
RWKV_LORA_G = 96
RWKV_GN_EPS = 64e-5
RWKV_CHUNK = 64
RWKV_BLOCK = 128
CHAIN_SKEW = 0

GDN_HEADS = 4
GDN_HEAD = 128
GDN_WIDTH = GDN_HEADS * GDN_HEAD
GDN_CONV = 5
GDN_CHUNK = 128
GDN_BLOCK = 256

COL_RKV = 0
COL_GQKV = 1536
COL_Z = 3072
COL_CQ = 3584
COL_CKV = 4096
COL_LORA = 4608
COL_KR = 4864
COL_GAB = 4992
IN_COLS_PAD = 5120
HALO = 8
VMEM_LIMIT = 56 * 1024 * 1024


def _dot(a, b):
    return jnp.dot(a, b, preferred_element_type=F32)


def _dot_nt(a, b):
    return lax.dot_general(a, b, (((1,), (1,)), ((), ())), preferred_element_type=F32)


def _dot_tn(a, b):
    return lax.dot_general(a, b, (((0,), (0,)), ((), ())), preferred_element_type=F32)


def _seg_sum(x, seg):
    hi = x.astype(BF16)
    lo = (x - hi.astype(F32)).astype(BF16)
    return _dot(hi, seg) + _dot(lo, seg)


def _dot_split3(ones_bf16, x):
    t1 = x.astype(BF16)
    r1 = x - t1.astype(F32)
    t2 = r1.astype(BF16)
    t3 = (r1 - t2.astype(F32)).astype(BF16)
    return _dot(ones_bf16, t1) + _dot(ones_bf16, t2) + _dot(ones_bf16, t3)


def _layer_norm(z, g, b):
    mu = jnp.mean(z, axis=-1, keepdims=True)
    zc = z - mu
    var = jnp.mean(zc * zc, axis=-1, keepdims=True)
    return zc * lax.rsqrt(var + 1e-5) * g + b


def _params(*sem):
    return pltpu.CompilerParams(dimension_semantics=sem, vmem_limit_bytes=VMEM_LIMIT)


def _mod_kernel(c_ref, w_ref, b_ref, o_ref):
    cv = c_ref[...]
    s = cv * jax.nn.sigmoid(cv)
    o_ref[0] = _dot(s.astype(BF16), w_ref[0].astype(BF16)) + b_ref[0]


def _modulation_all(cvec, w_mod, b_mod):
    L, D, N = w_mod.shape
    tn = 1024 if N % 1024 == 0 else 512
    return pl.pallas_call(
        _mod_kernel,
        grid=(L, N // tn),
        in_specs=[pl.BlockSpec((8, D), lambda l, j: (0, 0)),
                  pl.BlockSpec((1, D, tn), lambda l, j: (l, 0, j)),
                  pl.BlockSpec((1, 1, tn), lambda l, j: (l, 0, j))],
        out_specs=pl.BlockSpec((1, 8, tn), lambda l, j: (l, 0, j)),
        out_shape=jax.ShapeDtypeStruct((L, 8, N), F32),
        compiler_params=_params("parallel", "parallel"),
        name="adaln_modulation",
    )(cvec, w_mod, b_mod.reshape(L, 1, N))


def _inproj_kernel(x_ref, mod_ref, w_ref, o_ref):
    m = mod_ref[0]
    xm = x_ref[...] * (1.0 + m[1:2]) + m[0:1]
    o_ref[...] = _dot(xm.astype(BF16), w_ref[...])


def _in_proj(h, mod, w, n_lat, t_len, n_batch, tm):
    ntot, D = h.shape
    ncol = w.shape[1]
    tn = ncol // 2

    def mod_row(i):
        return jnp.where(i * tm < n_lat, (i * tm) // t_len, n_batch)

    return pl.pallas_call(
        _inproj_kernel,
        grid=(ncol // tn, ntot // tm),
        in_specs=[pl.BlockSpec((tm, D), lambda j, i: (i, 0)),
                  pl.BlockSpec((1, 8, D), lambda j, i: (mod_row(i), 0, 0)),
                  pl.BlockSpec((D, tn), lambda j, i: (0, j))],
        out_specs=pl.BlockSpec((tm, tn), lambda j, i: (i, j)),
        out_shape=jax.ShapeDtypeStruct((ntot, ncol), F32),
        compiler_params=_params("parallel", "parallel"),
        name="in_proj",
    )(h, mod, w)


def _mla_proj_kernel(cq_ref, ckv_ref, kr_ref, qn_ref, kn_ref, wq_ref, wkv_ref, cs1_ref, cs2_ref,
                     q_ref, k_ref, v_ref):
    cs1 = cs1_ref[...]
    cs2 = cs2_ref[...]

    def rms(x, g):
        return (x * lax.rsqrt(jnp.mean(x * x, axis=-1, keepdims=True) + 1e-6) * g).astype(BF16)

    def rope(g):
        return g * cs1 + pltpu.roll(g, MLA_ROPE, 1) * cs2

    oq = _dot(rms(cq_ref[...], qn_ref[...]), wq_ref[...])
    okv = _dot(rms(ckv_ref[...], kn_ref[...]), wkv_ref[...])
    krot = rope(kr_ref[...])
    for hd in range(MLA_HEADS):
        c0 = hd * MLA_QK_PAD
        qn = oq[:, c0:c0 + MLA_NOPE]
        qr = rope(oq[:, c0 + MLA_NOPE:c0 + MLA_QK_PAD])
        q_ref[hd] = (jnp.concatenate([qn, qr], axis=1) * Q_SCALE).astype(BF16)
        k_ref[hd] = jnp.concatenate([okv[:, c0:c0 + MLA_NOPE], krot], axis=1).astype(BF16)
        v_ref[hd] = okv[:, c0 + MLA_NOPE:c0 + MLA_QK_PAD].astype(BF16)


def _mla_proj(p, q_norm, kv_norm, wq, wkv, cs1, cs2, n_lat, t_len, tm):
    ntot = p.shape[0]
    H = MLA_HEADS
    npos = t_len // tm

    def pos(i):
        return jnp.where(i * tm < n_lat, i % npos, npos)

    return pl.pallas_call(
        _mla_proj_kernel,
        grid=(ntot // tm,),
        in_specs=[pl.BlockSpec((tm, MLA_RANK), lambda i: (i, COL_CQ // MLA_RANK)),
                  pl.BlockSpec((tm, MLA_RANK), lambda i: (i, COL_CKV // MLA_RANK)),
                  pl.BlockSpec((tm, LANES), lambda i: (i, COL_KR // LANES)),
                  pl.BlockSpec((1, MLA_RANK), lambda i: (0, 0)),
                  pl.BlockSpec((1, MLA_RANK), lambda i: (0, 0)),
                  pl.BlockSpec(wq.shape, lambda i: (0, 0)),
                  pl.BlockSpec(wkv.shape, lambda i: (0, 0)),
                  pl.BlockSpec((tm, LANES), lambda i: (pos(i), 0)),
                  pl.BlockSpec((tm, LANES), lambda i: (pos(i), 0))],
        out_specs=[pl.BlockSpec((H, tm, MLA_QK_PAD), lambda i: (0, i, 0)),
                   pl.BlockSpec((H, tm, MLA_QK_PAD), lambda i: (0, i, 0)),
                   pl.BlockSpec((H, tm, MLA_V), lambda i: (0, i, 0))],
        out_shape=[jax.ShapeDtypeStruct((H, ntot, MLA_QK_PAD), BF16),
                   jax.ShapeDtypeStruct((H, ntot, MLA_QK_PAD), BF16),
                   jax.ShapeDtypeStruct((H, ntot, MLA_V), BF16)],
        compiler_params=_params("parallel"),
        name="mla_proj",
    )(p, p, p, q_norm, kv_norm, wq, wkv, cs1, cs2)


def _softmax_pv(scores, values):
    def tiles(x):
        return [x[:, i * LANES:(i + 1) * LANES] for i in range(x.shape[1] // LANES)]

    m = jnp.max(functools.reduce(jnp.maximum, [t for s in scores for t in tiles(s)]), axis=-1, keepdims=True)
    ps = [jnp.exp2(s - m) for s in scores]
    l = jnp.sum(functools.reduce(jnp.add, [t for p in ps for t in tiles(p)]), axis=-1, keepdims=True)
    acc = functools.reduce(jnp.add, [_dot(p.astype(BF16), v) for p, v in zip(ps, values)])
    return acc / l


def _attn_kernel(q_ref, kl_ref, kc_ref, vl_ref, vc_ref, o_ref, *, nq_lat, n_chunks):
    qi = pl.program_id(2)
    heads = range(ATTN_HEADS_PER_STEP)

    @pl.when(qi < nq_lat)
    def _():
        scores, values = [], []
        for hd in heads:
            q = q_ref[hd]
            sc, va = [_dot_nt(q, kc_ref[hd])], [vc_ref[hd]]
            for c in range(n_chunks):
                sc.append(_dot_nt(q, kl_ref[hd, c * ATTN_TK:(c + 1) * ATTN_TK, :]))
                va.append(vl_ref[hd, c * ATTN_TK:(c + 1) * ATTN_TK, :])
            scores.append(sc)
            values.append(va)
        for hd in heads:
            o_ref[:, hd * MLA_V:(hd + 1) * MLA_V] = _softmax_pv(scores[hd], values[hd]).astype(o_ref.dtype)

    @pl.when(qi >= nq_lat)
    def _():
        for hd in heads:
            o = _softmax_pv([_dot_nt(q_ref[hd], kc_ref[hd])], [vc_ref[hd]])
            o_ref[:, hd * MLA_V:(hd + 1) * MLA_V] = o.astype(o_ref.dtype)


def _attention(qh, kh, vh, n_batch, t_len, ctx_len, with_ctx):
    H, ntot, _ = qh.shape
    n_lat = n_batch * t_len
    tq = ATTN_TQ
    hp = ATTN_HEADS_PER_STEP
    assert ctx_len == tq and t_len % ATTN_TK == 0 and H % hp == 0
    nq_lat = t_len // tq
    nq = nq_lat + (1 if with_ctx else 0)
    ctx_blk0 = n_lat // ctx_len

    def qrow(b, qi):
        return jnp.where(qi < nq_lat, b * nq_lat + qi, n_lat // tq + b)

    kern = functools.partial(_attn_kernel, nq_lat=nq_lat, n_chunks=t_len // ATTN_TK)
    return pl.pallas_call(
        kern,
        grid=(n_batch, H // hp, nq),
        in_specs=[pl.BlockSpec((hp, tq, MLA_QK_PAD), lambda b, h, qi: (h, qrow(b, qi), 0)),
                  pl.BlockSpec((hp, t_len, MLA_QK_PAD), lambda b, h, qi: (h, b, 0)),
                  pl.BlockSpec((hp, ctx_len, MLA_QK_PAD), lambda b, h, qi: (h, ctx_blk0 + b, 0)),
                  pl.BlockSpec((hp, t_len, MLA_V), lambda b, h, qi: (h, b, 0)),
                  pl.BlockSpec((hp, ctx_len, MLA_V), lambda b, h, qi: (h, ctx_blk0 + b, 0))],
        out_specs=pl.BlockSpec((tq, hp * MLA_V), lambda b, h, qi: (qrow(b, qi), h)),
        out_shape=jax.ShapeDtypeStruct((ntot, H * MLA_V), BF16),
        compiler_params=_params("parallel", "parallel", "arbitrary"),
        name="mla_attention",
    )(qh, kh, kh, vh, vh)


def _tri(n, reverse, transpose=False):
    i = lax.broadcasted_iota(jnp.int32, (n, n), 0)
    j = lax.broadcasted_iota(jnp.int32, (n, n), 1)
    if transpose:
        i, j = j, i
    return ((j >= i) if reverse else (j <= i)).astype(F32)


def _interleave(gens, skew):
    out = [None] * len(gens)
    live = set(range(len(gens)))
    tick = 0
    while live:
        for c in sorted(live):
            if tick < c * skew:
                continue
            try:
                next(gens[c])
            except StopIteration as stop:
                out[c] = stop.value
                live.discard(c)
        tick += 1
    return out


def _tri_inverse_consts(n, size):
    sub = SUBLANES
    ii = lax.broadcasted_iota(jnp.int32, (n, n), 0)
    jj = lax.broadcasted_iota(jnp.int32, (n, n), 1)
    diag = (ii // sub) == (jj // sub)
    lane = lax.broadcasted_iota(jnp.int32, (sub, n), 1) % sub
    row = lax.broadcasted_iota(jnp.int32, (sub, n), 0)
    offs = []
    s = sub
    while s < size:
        offs.append(((ii // (2 * s)) == (jj // (2 * s))) & ((ii // s) != (jj // s)))
        s *= 2
    return diag, lane, row, offs


def _diag_block_inverse(nmat, rev, consts):
    diag, lane, row, _ = consts
    n = nmat.shape[0]
    sub = SUBLANES
    nc = jnp.sum(jnp.where(diag, nmat, 0.0).reshape(n // sub, sub, n), axis=0)
    cols = []
    for j in range(sub):
        one = jnp.where(lane == j, nc, 0.0)
        up, dn = one, one
        for sh in (1, 2, 4):
            up = up + pltpu.roll(up, sh, 1)
            dn = dn + pltpu.roll(dn, n - sh, 1)
        cols.append(jnp.where(lane >= j, up, dn))
    x = (lane == row).astype(F32)
    for j in (range(sub - 1, 0, -1) if rev else range(sub - 1)):
        x = x + cols[j] * x[j:j + 1, :]
    return jnp.where(diag, jnp.broadcast_to(x[None], (n // sub, sub, n)).reshape(n, n), 0.0)


def _unit_tri_inverse_steps(nmat, rev, consts):
    x = _diag_block_inverse(nmat, rev, consts)
    yield
    for off in consts[3]:
        xb = x.astype(BF16)
        t1 = _dot(xb, jnp.where(off, nmat, 0.0).astype(BF16)).astype(BF16)
        yield
        x = x + _dot(t1, xb)
        yield
    return x


def _unit_tri_inverses(nmats, revs, size):
    consts = _tri_inverse_consts(nmats[0].shape[0], size)
    return _interleave([_unit_tri_inverse_steps(nm, rev, consts) for nm, rev in zip(nmats, revs)], CHAIN_SKEW)


def _rwkv_chunks(chains, revs):
    C = RWKV_CHUNK
    n2 = 2 * C
    tri = {rev: _tri(C, rev).astype(BF16) for rev in set(revs)}
    lane = lax.broadcasted_iota(jnp.int32, (C, LANES), 1)
    m0 = (lane < RWKV_HEAD).astype(F32)
    m1 = 1.0 - m0
    ii = lax.broadcasted_iota(jnp.int32, (n2, n2), 0)
    jj = lax.broadcasted_iota(jnp.int32, (n2, n2), 1)
    same = (ii // C) == (jj // C)
    il, jl = ii % C, jj % C
    strict = {rev: same & ((jl > il) if rev else (jl < il)) for rev in set(revs)}
    incl = {rev: same & ((jl >= il) if rev else (jl <= il)) for rev in set(revs)}
    inv_consts = _tri_inverse_consts(n2, C)

    def ex(x):
        return jnp.concatenate([x * m0, x * m1], axis=0)

    def chain(r, v, a, lw, kd, bd, rev):
        cum = _dot_split3(tri[rev], lw)
        tot = jnp.sum(lw, axis=0, keepdims=True)
        yield
        e_neg = jnp.exp(-cum)
        e_rem = jnp.exp(tot - cum)
        at, rt = ex(a * jnp.exp(cum - lw)), ex(r * jnp.exp(cum))
        bt, kt = ex(bd * e_neg), ex(kd * e_neg)
        bh, kh = ex(bd * e_rem).astype(BF16), ex(kd * e_rem).astype(BF16)
        ve = ex(v).astype(BF16)
        yield
        aa = _dot_nt(jnp.concatenate([at, rt], axis=0).astype(BF16), jnp.concatenate([bt, kt], axis=0).astype(BF16))
        yield
        a_ab = jnp.where(strict[rev], aa[:n2, :n2], 0.0)
        a_ak = jnp.where(strict[rev], aa[:n2, n2:], 0.0).astype(BF16)
        a_rb = jnp.where(incl[rev], aa[n2:, :n2], 0.0).astype(BF16)
        a_rk = jnp.where(incl[rev], aa[n2:, n2:], 0.0).astype(BF16)
        akv = _dot(a_ak, ve)
        y0b = _dot(a_rk, ve)
        yield
        tinv = yield from _unit_tri_inverse_steps(a_ab, rev, inv_consts)
        zb = _dot(tinv.astype(BF16), jnp.concatenate([at, akv], axis=1).astype(BF16)).astype(BF16)
        yield
        w2 = _dot(a_rb, zb)
        gmt = _dot_tn(zb[:, :LANES], bh).astype(BF16)
        s0t = _dot_tn(jnp.concatenate([zb[:, LANES:], ve], axis=0), jnp.concatenate([bh, kh], axis=0))
        yield
        rhat = (rt + w2[:, :LANES]).astype(BF16)
        y0 = w2[:, LANES:] + y0b
        return rhat, y0, gmt, s0t, jnp.exp(tot)

    return _interleave([chain(*c, rev) for c, rev in zip(chains, revs)], CHAIN_SKEW)


def _rwkv_apply(pre, states):
    C = RWKV_CHUNK
    sbs = [s.astype(BF16) for s in states]
    yes = [_dot_nt(rh, sb) + y0 for (rh, y0, _, _, _), sb in zip(pre, sbs)]
    ys = [ye[:C] + ye[C:] for ye in yes]
    new_states = [s * dec + _dot(sb, g) + s0 for (_, _, g, s0, dec), s, sb in zip(pre, states, sbs)]
    return ys, new_states


def _rwkv_scan_kernel(rf, rb, vf, vb, af, ab, lw0, lw1, kd0, kd1, b0, b1, yf_ref, yb_ref, s_ref):
    @pl.when(pl.program_id(1) == 0)
    def _():
        s_ref[...] = jnp.zeros_like(s_ref)

    C = RWKV_CHUNK
    nsub = rf.shape[0] // C
    dirs = ((rf, vf, af, lw0, kd0, b0, yf_ref), (rb, vb, ab, lw1, kd1, b1, yb_ref))
    npair = RWKV_WIDTH // LANES
    chains, revs, where = [], [], []
    for d, (r_, v_, a_, lw_, kd_, b_, y_) in enumerate(dirs):
        for j in range(npair):
            sl = slice(j * LANES, (j + 1) * LANES)
            for c in range(nsub):
                rows = slice(c * C, (c + 1) * C)
                chains.append(tuple(ref[rows, sl] for ref in (r_, v_, a_, lw_, kd_, b_)))
                revs.append(d == 1)
                where.append((d, j, c))
    pre = dict(zip(where, _rwkv_chunks(chains, revs)))
    keys = [(d, j) for d in range(2) for j in range(npair)]
    states = [s_ref[d, j] for d, j in keys]
    for step in range(nsub):
        sub = [step if d == 0 else nsub - 1 - step for d, _ in keys]
        ys, states = _rwkv_apply([pre[(d, j, c)] for (d, j), c in zip(keys, sub)], states)
        for (d, j), c, y in zip(keys, sub, ys):
            dirs[d][6][c * C:(c + 1) * C, j * LANES:(j + 1) * LANES] = y
    for (d, j), s_new in zip(keys, states):
        s_ref[d, j] = s_new


def _chunk_maps(n_batch, t_len, ctx_len, chunk):
    nctx, nlat = ctx_len // chunk, t_len // chunk
    base = n_batch * nlat

    def fwd(b, i):
        return jnp.where(i < nctx, base + b * nctx + i, b * nlat + i - nctx)

    def bwd(b, i):
        return jnp.where(i < nctx, base + b * nctx + (nctx - 1 - i), b * nlat + (nlat - 1 - (i - nctx)))

    return fwd, bwd, nctx + nlat


def _rwkv_scan(r, v, a, lw0, lw1, kd0, kd1, b0, b1, n_batch, t_len, ctx_len):
    ntot, W = r.shape
    C = RWKV_BLOCK
    fwd, bwd, nsteps = _chunk_maps(n_batch, t_len, ctx_len, C)
    fs = pl.BlockSpec((C, W), lambda b, i: (fwd(b, i), 0))
    bs = pl.BlockSpec((C, W), lambda b, i: (bwd(b, i), 0))
    return pl.pallas_call(
        _rwkv_scan_kernel,
        grid=(n_batch, nsteps),
        in_specs=[fs, bs, fs, bs, fs, bs, fs, bs, fs, bs, fs, bs],
        out_specs=[fs, bs],
        out_shape=[jax.ShapeDtypeStruct((ntot, W), F32)] * 2,
        scratch_shapes=[pltpu.VMEM((2, W // LANES, LANES, LANES), F32)],
        compiler_params=_params("parallel", "arbitrary"),
        name="rwkv7_scan",
    )(r, r, v, v, a, a, lw0, lw1, kd0, kd1, b0, b1)


def _gdn_scan_kernel(qf, qb, kf, kb, vf, vb, gf, gb, of_ref, ob_ref, s_ref):
    @pl.when(pl.program_id(1) == 0)
    def _():
        s_ref[...] = jnp.zeros_like(s_ref)

    C = GDN_CHUNK
    DH = GDN_HEAD
    ii = lax.broadcasted_iota(jnp.int32, (C, C), 0)
    jj = lax.broadcasted_iota(jnp.int32, (C, C), 1)
    strict = {False: jj < ii, True: jj > ii}
    incl = {False: jj <= ii, True: jj >= ii}
    nsub = qf.shape[0] // C
    dirs = ((qf, kf, vf, gf, of_ref), (qb, kb, vb, gb, ob_ref))
    qs, ks, vs, gcs, grs, gts, betas, revs, where = [], [], [], [], [], [], [], [], []
    for d, (q_, k_, v_, g_, o_) in enumerate(dirs):
        rev = d == 1
        for c in range(nsub):
            rows = slice(c * C, (c + 1) * C)
            gbv = g_[rows, :]
            gcol = jnp.dot(_tri(C, rev), gbv, precision=HIGHEST, preferred_element_type=F32)
            grow = jnp.dot(gbv.T, _tri(C, rev, transpose=True), precision=HIGHEST, preferred_element_type=F32)
            gtot = jnp.sum(gbv, axis=0, keepdims=True)
            for hd in range(GDN_HEADS):
                sl = slice(hd * DH, (hd + 1) * DH)
                cg = d * GDN_HEADS + hd
                cb = 2 * GDN_HEADS + cg
                qs.append(q_[rows, sl]); ks.append(k_[rows, sl]); vs.append(v_[rows, sl])
                gcs.append(gcol[:, cg:cg + 1]); grs.append(grow[cg:cg + 1, :]); gts.append(gtot[:, cg:cg + 1])
                betas.append(gbv[:, cb:cb + 1]); revs.append(rev); where.append((d, hd, c))

    decs = [jnp.exp(jnp.where(incl[rev], gc - gr, -1e30)) for gc, gr, rev in zip(gcs, grs, revs)]
    kbetas = [k * b for k, b in zip(ks, betas)]
    qks = [_dot_nt(jnp.concatenate([kb_, q], axis=0).astype(BF16), k.astype(BF16)) for kb_, q, k in zip(kbetas, qs, ks)]
    nmats = [jnp.where(strict[rev], -qk[:C] * dec, 0.0) for qk, dec, rev in zip(qks, decs, revs)]
    aints = [jnp.where(incl[rev], qk[C:] * dec, 0.0).astype(BF16) for qk, dec, rev in zip(qks, decs, revs)]
    egs = [jnp.exp(gc) for gc in gcs]
    rhss = [jnp.concatenate([v * b, kb_ * eg], axis=1).astype(BF16) for v, b, kb_, eg in zip(vs, betas, kbetas, egs)]
    khats = [(k * jnp.exp(gt - gc)).astype(BF16) for k, gt, gc in zip(ks, gts, gcs)]
    tinvs = _unit_tri_inverses(nmats, revs, C)
    sols = [_dot(ti.astype(BF16), rhs).astype(BF16) for ti, rhs in zip(tinvs, rhss)]
    qos = [_dot(ai, sol) for ai, sol in zip(aints, sols)]
    qhats = [(q * eg - qo[:, DH:]).astype(BF16) for q, eg, qo in zip(qs, egs, qos)]
    gss = [_dot_tn(kh, sol) for kh, sol in zip(khats, sols)]
    pre = dict(zip(where, zip(qhats, qos, gss, [jnp.exp(gt) for gt in gts])))

    keys = [(d, hd) for d in range(2) for hd in range(GDN_HEADS)]
    states = [s_ref[d, hd] for d, hd in keys]
    for step in range(nsub):
        sub = [step if d == 0 else nsub - 1 - step for d, _ in keys]
        cur = [pre[(d, hd, c)] for (d, hd), c in zip(keys, sub)]
        sbs = [s.astype(BF16) for s in states]
        os_ = [_dot(qh, sb) + qo[:, :DH] for (qh, qo, _, _), sb in zip(cur, sbs)]
        states = [s * dec - _dot(gs[:, DH:].astype(BF16), sb) + gs[:, :DH]
                  for (_, _, gs, dec), s, sb in zip(cur, states, sbs)]
        for (d, hd), c, o in zip(keys, sub, os_):
            dirs[d][4][c * C:(c + 1) * C, hd * DH:(hd + 1) * DH] = o
    for (d, hd), s_new in zip(keys, states):
        s_ref[d, hd] = s_new


def _gdn_scan(q, k, v, gb, n_batch, t_len, ctx_len):
    ntot, W = q.shape
    C = GDN_BLOCK
    fwd, bwd, nsteps = _chunk_maps(n_batch, t_len, ctx_len, C)
    fs = pl.BlockSpec((C, W), lambda b, i: (fwd(b, i), 0))
    bs = pl.BlockSpec((C, W), lambda b, i: (bwd(b, i), 0))
    gfs = pl.BlockSpec((C, LANES), lambda b, i: (fwd(b, i), 0))
    gbs = pl.BlockSpec((C, LANES), lambda b, i: (bwd(b, i), 0))
    return pl.pallas_call(
        _gdn_scan_kernel,
        grid=(n_batch, nsteps),
        in_specs=[fs, bs, fs, bs, fs, bs, gfs, gbs],
        out_specs=[fs, bs],
        out_shape=[jax.ShapeDtypeStruct((ntot, W), F32)] * 2,
        scratch_shapes=[pltpu.VMEM((2, GDN_HEADS, GDN_HEAD, GDN_HEAD), F32)],
        compiler_params=_params("parallel", "arbitrary"),
        name="gdn_scan",
    )(q, q, k, k, v, v, gb, gb)


def _outproj_kernel(x_ref, a_ref, yf_ref, yb_ref, r_ref, v_ref, kd0_ref, kd1_ref, g_ref, of_ref, ob_ref, z_ref,
                    rk_ref, gng_ref, gnb_ref, gdn_ref, seg_ref, w_ref, mod_ref, lng_ref, lnb_ref, o_ref, *, alpha):
    seg = seg_ref[...]
    y = yf_ref[...] + yb_ref[...]
    yc = y - _seg_sum(y, seg) * (1.0 / RWKV_HEAD)
    var = _seg_sum(yc * yc, seg) * (1.0 / RWKV_HEAD)
    yn = yc * lax.rsqrt(var + RWKV_GN_EPS) * gng_ref[...] + gnb_ref[...]
    bonus = _seg_sum(r_ref[...] * (kd0_ref[...] + kd1_ref[...]) * rk_ref[...], seg) * v_ref[...]
    b_mix = ((yn + bonus) * g_ref[...]).astype(BF16)
    o = of_ref[...] + ob_ref[...]
    z = z_ref[...]
    parts = []
    for hd in range(GDN_HEADS):
        oh = o[:, hd * GDN_HEAD:(hd + 1) * GDN_HEAD]
        parts.append(oh * lax.rsqrt(jnp.mean(oh * oh, axis=-1, keepdims=True) + 1e-6))
    c_mix = (jnp.concatenate(parts, axis=1) * gdn_ref[...] * (z * jax.nn.sigmoid(z))).astype(BF16)

    na, nb = a_ref.shape[1], RWKV_WIDTH
    acc = _dot(a_ref[...], w_ref[0:na, :])
    acc += _dot(b_mix, w_ref[na:na + nb, :])
    acc += _dot(c_mix, w_ref[na + nb:, :])
    gate = mod_ref[0][2:3]
    o_ref[...] = _layer_norm(alpha * x_ref[...] + gate * acc, lng_ref[...], lnb_ref[...])


def _out_proj(h, a, rw, gd, p, r_k, gn_g, gn_b, gdn_norm, seg, w, mod, ln_g, ln_b,
              n_rows, n_lat, t_len, n_batch, tm, alpha):
    D = h.shape[1]
    W = RWKV_WIDTH

    def mod_row(i):
        return jnp.where(i * tm < n_lat, (i * tm) // t_len, n_batch)

    tok = pl.BlockSpec((tm, W), lambda i: (i, 0))
    vec = pl.BlockSpec((1, W), lambda i: (0, 0))
    return pl.pallas_call(
        functools.partial(_outproj_kernel, alpha=alpha),
        grid=(n_rows // tm,),
        in_specs=[pl.BlockSpec((tm, D), lambda i: (i, 0)),
                  pl.BlockSpec((tm, a.shape[1]), lambda i: (i, 0))]
                 + [tok] * 9
                 + [pl.BlockSpec((tm, W), lambda i: (i, COL_Z // W)),
                    vec, vec, vec, vec,
                    pl.BlockSpec(seg.shape, lambda i: (0, 0)),
                    pl.BlockSpec(w.shape, lambda i: (0, 0)),
                    pl.BlockSpec((1, 8, D), lambda i: (mod_row(i), 0, 0)),
                    pl.BlockSpec((1, D), lambda i: (0, 0)),
                    pl.BlockSpec((1, D), lambda i: (0, 0))],
        out_specs=pl.BlockSpec((tm, D), lambda i: (i, 0)),
        out_shape=jax.ShapeDtypeStruct((n_rows, D), F32),
        compiler_params=_params("parallel"),
        name="out_proj_ln",
    )(h, a, *rw, *gd, p, r_k, gn_g, gn_b, gdn_norm, seg, w, mod, ln_g, ln_b)


def _ffn_kernel(x_ref, mod_ref, wg_ref, wu_ref, wd_ref, g_ref, beta_ref, o_ref, xm_ref, acc_ref, *, alpha):
    f = pl.program_id(1)

    @pl.when(f == 0)
    def _():
        m = mod_ref[0]
        xm_ref[...] = (x_ref[...] * (1.0 + m[4:5]) + m[3:4]).astype(BF16)
        acc_ref[...] = jnp.zeros_like(acc_ref)

    xm = xm_ref[...]
    gate = _dot(xm, wg_ref[...])
    up = _dot(xm, wu_ref[...])
    act = gate * jax.nn.sigmoid(gate) * up
    acc_ref[...] += _dot(act.astype(BF16), wd_ref[...])

    @pl.when(f == pl.num_programs(1) - 1)
    def _():
        m = mod_ref[0]
        o_ref[...] = _layer_norm(alpha * x_ref[...] + m[5:6] * acc_ref[...], g_ref[...], beta_ref[...])


def _ffn(h, mod, wg, wu, wd, ln_g, ln_b, n_rows, n_lat, t_len, n_batch, tm, tf, alpha):
    D = h.shape[1]
    F = wg.shape[1]

    def mod_row(i):
        return jnp.where(i * tm < n_lat, (i * tm) // t_len, n_batch)

    return pl.pallas_call(
        functools.partial(_ffn_kernel, alpha=alpha),
        grid=(n_rows // tm, F // tf),
        in_specs=[pl.BlockSpec((tm, D), lambda i, f: (i, 0)),
                  pl.BlockSpec((1, 8, D), lambda i, f: (mod_row(i), 0, 0)),
                  pl.BlockSpec((D, tf), lambda i, f: (0, f)),
                  pl.BlockSpec((D, tf), lambda i, f: (0, f)),
                  pl.BlockSpec((tf, D), lambda i, f: (f, 0)),
                  pl.BlockSpec((1, D), lambda i, f: (0, 0)),
                  pl.BlockSpec((1, D), lambda i, f: (0, 0))],
        out_specs=pl.BlockSpec((tm, D), lambda i, f: (i, 0)),
        out_shape=jax.ShapeDtypeStruct((n_rows, D), F32),
        scratch_shapes=[pltpu.VMEM((tm, D), BF16), pltpu.VMEM((tm, D), F32)],
        compiler_params=_params("parallel", "arbitrary"),
        name="ffn_ln",
    )(h, mod, wg, wu, wd, ln_g, ln_b)


def _in_proj_columns():
    kr0 = 2 * MLA_RANK
    swap = np.concatenate([np.arange(16, 32), np.arange(0, 16), np.arange(48, 64), np.arange(32, 48)])
    pr0 = kr0 + MLA_ROPE
    n_lora = 2 * RWKV_LORA_W + 2 * RWKV_LORA_A + RWKV_LORA_G
    pg0 = pr0 + 3 * RWKV_WIDTH + n_lora
    nh4 = 4 * GDN_HEADS
    idx = -np.ones(IN_COLS_PAD, np.int64)

    def put(dst, src):
        idx[dst:dst + len(src)] = src

    put(COL_RKV, pr0 + np.arange(3 * RWKV_WIDTH))
    put(COL_GQKV, pg0 + np.arange(3 * GDN_WIDTH))
    put(COL_Z, pg0 + 3 * GDN_WIDTH + np.arange(GDN_WIDTH))
    put(COL_CQ, np.arange(MLA_RANK))
    put(COL_CKV, MLA_RANK + np.arange(MLA_RANK))
    put(COL_LORA, pr0 + 3 * RWKV_WIDTH + np.arange(n_lora))
    put(COL_KR, kr0 + np.arange(MLA_ROPE))
    put(COL_KR + MLA_ROPE, kr0 + swap)
    put(COL_GAB, pg0 + 4 * GDN_WIDTH + np.arange(nh4))
    return idx


def _take_cols(w, idx):
    safe = jnp.asarray(np.maximum(idx, 0), jnp.int32)
    mask = jnp.asarray(idx >= 0)
    return jnp.where(mask, jnp.take(w, safe, axis=-1), 0.0)


def _uq_columns():
    swap = np.concatenate([np.arange(16, 32), np.arange(0, 16), np.arange(48, 64), np.arange(32, 48)])
    per = MLA_NOPE + MLA_ROPE
    out = []
    for hd in range(MLA_HEADS):
        out += [hd * per + np.arange(MLA_NOPE), hd * per + MLA_NOPE + np.arange(MLA_ROPE), hd * per + MLA_NOPE + swap]
    return np.concatenate(out)


def _rope_tables(t_len, tm):
    rows = t_len // GRID_W
    row = jnp.repeat(jnp.arange(rows), GRID_W)
    col = jnp.tile(jnp.arange(GRID_W), rows)
    n_freq = MLA_ROPE // 4
    inv = ROPE_THETA ** (-jnp.arange(n_freq, dtype=F32) / n_freq)
    ar = row.astype(F32)[:, None] * inv
    ac = col.astype(F32)[:, None] * inv
    cos = jnp.concatenate([jnp.cos(ar), jnp.cos(ar), jnp.cos(ac), jnp.cos(ac)], axis=1)
    sin = jnp.concatenate([-jnp.sin(ar), jnp.sin(ar), -jnp.sin(ac), jnp.sin(ac)], axis=1)
    zer = jnp.zeros((t_len, LANES - MLA_ROPE), F32)
    cs1 = jnp.concatenate([cos, zer], axis=1)
    cs2 = jnp.concatenate([sin, zer], axis=1)
    ident = jnp.concatenate([jnp.ones((tm, MLA_ROPE), F32), jnp.zeros((tm, LANES - MLA_ROPE), F32)], axis=1)
    return jnp.concatenate([cs1, ident], axis=0), jnp.concatenate([cs2, jnp.zeros((tm, LANES), F32)], axis=0)


def _fill_halo(ext_ref, cur_ref, prev_ref, next_ref, first, last):
    tm = cur_ref.shape[0]
    ext_ref[0:HALO, :] = jnp.where(first, 0.0, prev_ref[...])
    ext_ref[HALO:HALO + tm, :] = cur_ref[...]
    ext_ref[HALO + tm:, :] = jnp.where(last, 0.0, next_ref[...])


def _seq_edges(tm, n_lat, t_len, ctx_len):
    t0 = pl.program_id(0) * tm
    lat = t0 < n_lat
    first = jnp.where(lat, t0 % t_len == 0, (t0 - n_lat) % ctx_len == 0)
    last = jnp.where(lat, (t0 + tm) % t_len == 0, (t0 - n_lat + tm) % ctx_len == 0)
    return first, last


def _softplus(x):
    return jnp.maximum(x, 0.0) + jnp.log(1.0 + jnp.exp(-jnp.abs(x)))


def _rwkv_prep_kernel(rkv_ref, rkv_p, rkv_n, lg_ref, lg_p, lg_n, mu_rkv, mu_lg, w0_ref, a0_ref, w2_ref, a2_ref,
                      g2_ref, kk_ref, ka_ref, seg_ref,
                      r_o, v_o, a_o, lw0_o, lw1_o, kd0_o, kd1_o, b0_o, b1_o, g_o, ext_rkv, ext_lg,
                      *, n_lat, t_len, ctx_len):
    tm = rkv_ref.shape[0]
    W = RWKV_WIDTH
    first, last = _seq_edges(tm, n_lat, t_len, ctx_len)
    _fill_halo(ext_rkv, rkv_ref, rkv_p, rkv_n, first, last)
    _fill_halo(ext_lg, lg_ref, lg_p, lg_n, first, last)

    def mixed(ext, mu):
        cur = ext[HALO:HALO + tm, :]
        sh = 0.5 * (ext[HALO - 1:HALO - 1 + tm, :] + ext[HALO + 1:HALO + 1 + tm, :])
        return cur + (sh - cur) * mu[...]

    p = mixed(ext_rkv, mu_rkv)
    lg = mixed(ext_lg, mu_lg)
    r, k, v = p[:, :W], p[:, W:2 * W], p[:, 2 * W:]
    lora = lg[:, :LANES]
    tw = jnp.tanh(lora).astype(BF16)
    ab = lora.astype(BF16)
    kkr = k * kk_ref[...]
    kk = kkr * lax.rsqrt(_seg_sum(kkr * kkr, seg_ref[...]) + 1e-12)
    r_o[...] = r
    v_o[...] = v
    a_o[...] = -kk
    g_o[...] = _dot(jax.nn.sigmoid(lg[:, LANES:]).astype(BF16), g2_ref[...])
    for d, (lw_o, kd_o, b_o) in enumerate(((lw0_o, kd0_o, b0_o), (lw1_o, kd1_o, b1_o))):
        w_log = -_softplus(-(w0_ref[d:d + 1, :] + _dot(tw, w2_ref[d]))) - 0.5
        lw_o[...] = -jnp.exp(w_log)
        a_lr = jax.nn.sigmoid(a0_ref[d:d + 1, :] + _dot(ab, a2_ref[d]))
        kd_o[...] = k * (1.0 + (a_lr - 1.0) * ka_ref[...])
        b_o[...] = kk * a_lr


def _halo_specs(tm, width, col, ntot):
    nb = tm // HALO
    last_blk = ntot // HALO - 1
    return [pl.BlockSpec((tm, width), lambda i: (i, col)),
            pl.BlockSpec((HALO, width), lambda i: (jnp.maximum(i * nb - 1, 0), col)),
            pl.BlockSpec((HALO, width), lambda i: (jnp.minimum((i + 1) * nb, last_blk), col))]


def _rwkv_prep(p, mu_rkv, mu_lg, w0, a0, w2p, a2p, g2p, k_k, k_a, seg, n_lat, t_len, ctx_len, tm):
    ntot = p.shape[0]
    W = RWKV_WIDTH
    full = lambda arr: pl.BlockSpec(arr.shape, lambda i: (0,) * arr.ndim)
    params = (mu_rkv, mu_lg, w0, a0, w2p, a2p, g2p, k_k, k_a, seg)
    return pl.pallas_call(
        functools.partial(_rwkv_prep_kernel, n_lat=n_lat, t_len=t_len, ctx_len=ctx_len),
        grid=(ntot // tm,),
        in_specs=_halo_specs(tm, 3 * W, COL_RKV // (3 * W), ntot) + _halo_specs(tm, 2 * LANES, COL_LORA // (2 * LANES), ntot)
                 + [full(a) for a in params],
        out_specs=[pl.BlockSpec((tm, W), lambda i: (i, 0))] * 10,
        out_shape=[jax.ShapeDtypeStruct((ntot, W), F32)] * 10,
        scratch_shapes=[pltpu.VMEM((tm + 2 * HALO, 3 * W), F32), pltpu.VMEM((tm + 2 * HALO, 2 * LANES), F32)],
        compiler_params=_params("parallel"),
        name="rwkv7_prep",
    )(p, p, p, p, p, p, *params)


def _gdn_prep_kernel(x_ref, x_p, x_n, ab_ref, cw_ref, alog_ref, dt_ref, q_o, k_o, v_o, gb_o, ext,
                     *, n_lat, t_len, ctx_len):
    tm = x_ref.shape[0]
    W = GDN_WIDTH
    first, last = _seq_edges(tm, n_lat, t_len, ctx_len)
    _fill_halo(ext, x_ref, x_p, x_n, first, last)
    half = GDN_CONV // 2
    conv = functools.reduce(jnp.add, [ext[HALO - half + j:HALO - half + j + tm, :] * cw_ref[j:j + 1, :]
                                      for j in range(GDN_CONV)])
    act = conv * jax.nn.sigmoid(conv)

    def l2n(x):
        parts = []
        for hd in range(GDN_HEADS):
            xh = x[:, hd * GDN_HEAD:(hd + 1) * GDN_HEAD]
            parts.append(xh * lax.rsqrt(jnp.sum(xh * xh, axis=-1, keepdims=True) + 1e-12))
        return jnp.concatenate(parts, axis=1)

    q_o[...] = l2n(act[:, :W]) * GDN_HEAD ** -0.5
    k_o[...] = l2n(act[:, W:2 * W])
    v_o[...] = act[:, 2 * W:]
    ab = ab_ref[...]
    lane = lax.broadcasted_iota(jnp.int32, ab.shape, 1)
    nh2 = 2 * GDN_HEADS
    glog = -jnp.exp(alog_ref[...]) * _softplus(ab + dt_ref[...])
    gb_o[...] = jnp.where(lane < nh2, glog, jnp.where(lane < 2 * nh2, jax.nn.sigmoid(ab), 0.0))


def _gdn_prep(p, conv_w, a_log, dt_bias, n_lat, t_len, ctx_len, tm):
    ntot = p.shape[0]
    W = GDN_WIDTH
    full = lambda arr: pl.BlockSpec(arr.shape, lambda i: (0,) * arr.ndim)
    return pl.pallas_call(
        functools.partial(_gdn_prep_kernel, n_lat=n_lat, t_len=t_len, ctx_len=ctx_len),
        grid=(ntot // tm,),
        in_specs=_halo_specs(tm, 3 * W, COL_GQKV // (3 * W), ntot)
                 + [pl.BlockSpec((tm, LANES), lambda i: (i, COL_GAB // LANES)), full(conv_w), full(a_log), full(dt_bias)],
        out_specs=[pl.BlockSpec((tm, W), lambda i: (i, 0))] * 3 + [pl.BlockSpec((tm, LANES), lambda i: (i, 0))],
        out_shape=[jax.ShapeDtypeStruct((ntot, W), F32)] * 3 + [jax.ShapeDtypeStruct((ntot, LANES), F32)],
        scratch_shapes=[pltpu.VMEM((tm + 2 * HALO, 3 * W), F32)],
        compiler_params=_params("parallel"),
        name="gdn_prep",
    )(p, p, p, p, conv_w, a_log, dt_bias)


def kernel(x, c, ctx, c_ctx, w_mod, b_mod, w_in, mla_q_norm, mla_kv_norm, mla_w_uq, mla_w_ukv, rwkv_mu, rwkv_w0, rwkv_w2, rwkv_a0, rwkv_a2, rwkv_g2, rwkv_k_k, rwkv_k_a, rwkv_r_k, rwkv_gn_g, rwkv_gn_b, gdn_conv, gdn_a_log, gdn_dt_bias, gdn_norm, w_out, ln1_g, ln1_b, ffn_w_gate, ffn_w_up, ffn_w_down, ln2_g, ln2_b):
    B, T, D = x.shape
    CTX = ctx.shape[1]
    L = w_mod.shape[0]
    n_lat, n_ctx = B * T, B * CTX
    ntot = n_lat + n_ctx
    alpha = (2.0 * L) ** 0.25
    tm = 256
    tm_big = 512 if (n_lat % 512 == 0 and n_ctx % 512 == 0 and T % 512 == 0) else tm
    tf = 512
    assert T % tm == 0 and n_ctx % tm == 0 and B + 1 <= 8

    cvec = jnp.concatenate([c, c_ctx[None], jnp.zeros((8 - B - 1, D), F32)], axis=0)
    mod_all = _modulation_all(cvec, w_mod, b_mod).reshape(L, 8, 6, D)[:, :B + 1]
    mod_all = jnp.pad(mod_all, ((0, 0), (0, 0), (0, 2), (0, 0)))

    in_idx = _in_proj_columns()
    w_in_p = _take_cols(w_in, in_idx).astype(BF16)
    uq_idx = _uq_columns()
    w_uq_p = jnp.take(mla_w_uq, jnp.asarray(uq_idx, jnp.int32), axis=-1).astype(BF16)
    w_ukv_b = mla_w_ukv.astype(BF16)
    w_out_b = w_out.astype(BF16)
    wg_b, wu_b, wd_b = ffn_w_gate.astype(BF16), ffn_w_up.astype(BF16), ffn_w_down.astype(BF16)
    cs1, cs2 = _rope_tables(T, tm)

    W = RWKV_WIDTH
    mu_rkv = rwkv_mu[:, None, :3 * W]
    mu_lg = jnp.pad(rwkv_mu[:, None, 3 * W:], ((0, 0), (0, 0), (0, 2 * LANES - (rwkv_mu.shape[1] - 3 * W))))
    lw, la = RWKV_LORA_W, RWKV_LORA_A
    w2p = jnp.zeros((L, 2, LANES, W), F32)
    a2p = jnp.zeros((L, 2, LANES, W), F32)
    for d in range(2):
        w2p = w2p.at[:, d, d * lw:(d + 1) * lw].set(rwkv_w2[:, d])
        a2p = a2p.at[:, d, 2 * lw + d * la:2 * lw + (d + 1) * la].set(rwkv_a2[:, d])
    w2p, a2p = w2p.astype(BF16), a2p.astype(BF16)
    g2p = jnp.pad(rwkv_g2, ((0, 0), (0, LANES - RWKV_LORA_G), (0, 0))).astype(BF16)
    lane_head = np.arange(W) // RWKV_HEAD
    seg = jnp.asarray(lane_head[:, None] == lane_head[None, :], BF16)
    conv_p = jnp.pad(gdn_conv, ((0, 0), (0, 8 - GDN_CONV), (0, 0)))
    nh2 = 2 * GDN_HEADS
    alog_p = jnp.pad(gdn_a_log.reshape(L, 1, nh2), ((0, 0), (0, 0), (0, LANES - nh2)))
    dt_p = jnp.pad(gdn_dt_bias.reshape(L, 1, nh2), ((0, 0), (0, 0), (0, LANES - nh2)))
    r_k = rwkv_r_k.reshape(L, 1, W)
    gdn_norm_t = jnp.tile(gdn_norm, (1, GDN_HEADS))[:, None, :]

    h = jnp.concatenate([x.reshape(n_lat, D), ctx.reshape(n_ctx, D)], axis=0)
    for l in range(L):
        last = l == L - 1
        mod = mod_all[l]
        p = _in_proj(h, mod, w_in_p[l], n_lat, T, B, tm_big)

        qh, kh, vh = _mla_proj(p, mla_q_norm[l][None], mla_kv_norm[l][None], w_uq_p[l], w_ukv_b[l], cs1, cs2,
                               n_lat, T, tm)
        a_mix = _attention(qh, kh, vh, B, T, CTX, with_ctx=not last)

        r, v, a, lw0, lw1, kd0, kd1, b0, b1, g = _rwkv_prep(
            p, mu_rkv[l], mu_lg[l], rwkv_w0[l], rwkv_a0[l], w2p[l], a2p[l], g2p[l],
            rwkv_k_k[l][None], rwkv_k_a[l][None], seg, n_lat, T, CTX, tm)
        yf, yb = _rwkv_scan(r, v, a, lw0, lw1, kd0, kd1, b0, b1, B, T, CTX)

        gq, gk, gv, gb = _gdn_prep(p, conv_p[l], alog_p[l], dt_p[l], n_lat, T, CTX, tm)
        of, ob = _gdn_scan(gq, gk, gv, gb, B, T, CTX)

        n_rows = n_lat if last else ntot
        h1 = _out_proj(h, a_mix, (yf, yb, r, v, kd0, kd1, g), (of, ob), p, r_k[l], rwkv_gn_g[l][None],
                       rwkv_gn_b[l][None], gdn_norm_t[l], seg, w_out_b[l], mod, ln1_g[l][None], ln1_b[l][None],
                       n_rows, n_lat, T, B, tm, alpha)
        h = _ffn(h1, mod, wg_b[l], wu_b[l], wd_b[l], ln2_g[l][None], ln2_b[l][None],
                 n_rows, n_lat, T, B, tm_big, tf, alpha)
    return h[:n_lat].reshape(B, T, D)
```

```python
import functools
import math

import numpy as np
import jax
import jax.numpy as jnp
from jax import lax
from jax.experimental import pallas as pl
from jax.experimental.pallas import tpu as pltpu

F32 = jnp.float32
BF16 = jnp.bfloat16
HIGHEST = lax.Precision.HIGHEST

LANES = 128
SUBLANES = 8
GRID_W = 64
ROPE_THETA = 10000.0

MLA_HEADS = 8
MLA_RANK = 512
MLA_NOPE = 128
MLA_ROPE = 64
MLA_V = 128
MLA_QK_PAD = 256
ATTN_SCALE = (MLA_NOPE + MLA_ROPE) ** -0.5
Q_SCALE = ATTN_SCALE * math.log2(math.e)
ATTN_TQ = 256
ATTN_TK = 512
ATTN_HEADS_PER_STEP = 4

RWKV_HEADS = 8
RWKV_HEAD = 64
RWKV_WIDTH = RWKV_HEADS * RWKV_HEAD
RWKV_LORA_W = 32
RWKV_LORA_A = 32
RWKV_LORA_G = 96
RWKV_GN_EPS = 64e-5
RWKV_CHUNK = 64
RWKV_BLOCK = 128
CHAIN_SKEW = 0

GDN_HEADS = 4
GDN_HEAD = 128
GDN_WIDTH = GDN_HEADS * GDN_HEAD
GDN_CONV = 5
GDN_CHUNK = 128
GDN_BLOCK = 256

COL_RKV = 0
COL_GQKV = 1536
COL_Z = 3072
COL_CQ = 3584
COL_CKV = 4096
COL_LORA = 4608
COL_KR = 4864
COL_GAB = 4992
IN_COLS_PAD = 5120
HALO = 8
VMEM_LIMIT = 56 * 1024 * 1024


def _dot(a, b):
    return jnp.dot(a, b, preferred_element_type=F32)


def _dot_nt(a, b):
    return lax.dot_general(a, b, (((1,), (1,)), ((), ())), preferred_element_type=F32)


def _dot_tn(a, b):
    return lax.dot_general(a, b, (((0,), (0,)), ((), ())), preferred_element_type=F32)


def _seg_sum(x, seg):
    return _dot(x.astype(BF16), seg)


def _dot_split3(ones_bf16, x):
    t1 = x.astype(BF16)
    r1 = x - t1.astype(F32)
    t2 = r1.astype(BF16)
    t3 = (r1 - t2.astype(F32)).astype(BF16)
    return _dot(ones_bf16, t1) + _dot(ones_bf16, t2) + _dot(ones_bf16, t3)


def _layer_norm(z, g, b):
    mu = jnp.mean(z, axis=-1, keepdims=True)
    zc = z - mu
    var = jnp.mean(zc * zc, axis=-1, keepdims=True)
    return zc * lax.rsqrt(var + 1e-5) * g + b


def _params(*sem):
    return pltpu.CompilerParams(dimension_semantics=sem, vmem_limit_bytes=VMEM_LIMIT)


def _mod_kernel(c_ref, w_ref, b_ref, o_ref):
    cv = c_ref[...]
    s = cv * jax.nn.sigmoid(cv)
    o_ref[0] = _dot(s.astype(BF16), w_ref[0].astype(BF16)) + b_ref[0]


def _modulation_all(cvec, w_mod, b_mod):
    L, D, N = w_mod.shape
    tn = 1024 if N % 1024 == 0 else 512
    return pl.pallas_call(
        _mod_kernel,
        grid=(L, N // tn),
        in_specs=[pl.BlockSpec((8, D), lambda l, j: (0, 0)),
                  pl.BlockSpec((1, D, tn), lambda l, j: (l, 0, j)),
                  pl.BlockSpec((1, 1, tn), lambda l, j: (l, 0, j))],
        out_specs=pl.BlockSpec((1, 8, tn), lambda l, j: (l, 0, j)),
        out_shape=jax.ShapeDtypeStruct((L, 8, N), F32),
        compiler_params=_params("parallel", "parallel"),
        name="adaln_modulation",
    )(cvec, w_mod, b_mod.reshape(L, 1, N))


def _inproj_kernel(x_ref, mod_ref, w_ref, o_ref):
    m = mod_ref[0]
    xm = x_ref[...] * (1.0 + m[1:2]) + m[0:1]
    o_ref[...] = _dot(xm.astype(BF16), w_ref[...])


def _in_proj(h, mod, w, n_lat, t_len, n_batch, tm):
    ntot, D = h.shape
    ncol = w.shape[1]
    tn = ncol // 2

    def mod_row(i):
        return jnp.where(i * tm < n_lat, (i * tm) // t_len, n_batch)

    return pl.pallas_call(
        _inproj_kernel,
        grid=(ncol // tn, ntot // tm),
        in_specs=[pl.BlockSpec((tm, D), lambda j, i: (i, 0)),
                  pl.BlockSpec((1, 8, D), lambda j, i: (mod_row(i), 0, 0)),
                  pl.BlockSpec((D, tn), lambda j, i: (0, j))],
        out_specs=pl.BlockSpec((tm, tn), lambda j, i: (i, j)),
        out_shape=jax.ShapeDtypeStruct((ntot, ncol), F32),
        compiler_params=_params("parallel", "parallel"),
        name="in_proj",
    )(h, mod, w)


def _mla_proj_kernel(cq_ref, ckv_ref, kr_ref, qn_ref, kn_ref, wq_ref, wkv_ref, cs1_ref, cs2_ref,
                     q_ref, k_ref, v_ref):
    cs1 = cs1_ref[...]
    cs2 = cs2_ref[...]

    def rms(x, g):
        return (x * lax.rsqrt(jnp.mean(x * x, axis=-1, keepdims=True) + 1e-6) * g).astype(BF16)

    def rope(g):
        return g * cs1 + pltpu.roll(g, MLA_ROPE, 1) * cs2

    oq = _dot(rms(cq_ref[...], qn_ref[...]), wq_ref[...])
    okv = _dot(rms(ckv_ref[...], kn_ref[...]), wkv_ref[...])
    krot = rope(kr_ref[...])
    for hd in range(MLA_HEADS):
        c0 = hd * MLA_QK_PAD
        qn = oq[:, c0:c0 + MLA_NOPE]
        qr = rope(oq[:, c0 + MLA_NOPE:c0 + MLA_QK_PAD])
        q_ref[hd] = (jnp.concatenate([qn, qr], axis=1) * Q_SCALE).astype(BF16)
        k_ref[hd] = jnp.concatenate([okv[:, c0:c0 + MLA_NOPE], krot], axis=1).astype(BF16)
        v_ref[hd] = okv[:, c0 + MLA_NOPE:c0 + MLA_QK_PAD].astype(BF16)


def _mla_proj(p, q_norm, kv_norm, wq, wkv, cs1, cs2, n_lat, t_len, tm):
    ntot = p.shape[0]
    H = MLA_HEADS
    npos = t_len // tm

    def pos(i):
        return jnp.where(i * tm < n_lat, i % npos, npos)

    return pl.pallas_call(
        _mla_proj_kernel,
        grid=(ntot // tm,),
        in_specs=[pl.BlockSpec((tm, MLA_RANK), lambda i: (i, COL_CQ // MLA_RANK)),
                  pl.BlockSpec((tm, MLA_RANK), lambda i: (i, COL_CKV // MLA_RANK)),
                  pl.BlockSpec((tm, LANES), lambda i: (i, COL_KR // LANES)),
                  pl.BlockSpec((1, MLA_RANK), lambda i: (0, 0)),
                  pl.BlockSpec((1, MLA_RANK), lambda i: (0, 0)),
                  pl.BlockSpec(wq.shape, lambda i: (0, 0)),
                  pl.BlockSpec(wkv.shape, lambda i: (0, 0)),
                  pl.BlockSpec((tm, LANES), lambda i: (pos(i), 0)),
                  pl.BlockSpec((tm, LANES), lambda i: (pos(i), 0))],
        out_specs=[pl.BlockSpec((H, tm, MLA_QK_PAD), lambda i: (0, i, 0)),
                   pl.BlockSpec((H, tm, MLA_QK_PAD), lambda i: (0, i, 0)),
                   pl.BlockSpec((H, tm, MLA_V), lambda i: (0, i, 0))],
        out_shape=[jax.ShapeDtypeStruct((H, ntot, MLA_QK_PAD), BF16),
                   jax.ShapeDtypeStruct((H, ntot, MLA_QK_PAD), BF16),
                   jax.ShapeDtypeStruct((H, ntot, MLA_V), BF16)],
        compiler_params=_params("parallel"),
        name="mla_proj",
    )(p, p, p, q_norm, kv_norm, wq, wkv, cs1, cs2)


def _softmax_pv(scores, values):
    def tiles(x):
        return [x[:, i * LANES:(i + 1) * LANES] for i in range(x.shape[1] // LANES)]

    m = jnp.max(functools.reduce(jnp.maximum, [t for s in scores for t in tiles(s)]), axis=-1, keepdims=True)
    ps = [jnp.exp2(s - m) for s in scores]
    l = jnp.sum(functools.reduce(jnp.add, [t for p in ps for t in tiles(p)]), axis=-1, keepdims=True)
    acc = functools.reduce(jnp.add, [_dot(p.astype(BF16), v) for p, v in zip(ps, values)])
    return acc / l


def _attn_kernel(q_ref, kl_ref, kc_ref, vl_ref, vc_ref, o_ref, *, nq_lat, n_chunks):
    qi = pl.program_id(2)
    heads = range(ATTN_HEADS_PER_STEP)

    @pl.when(qi < nq_lat)
    def _():
        scores, values = [], []
        for hd in heads:
            q = q_ref[hd]
            sc, va = [_dot_nt(q, kc_ref[hd])], [vc_ref[hd]]
            for c in range(n_chunks):
                sc.append(_dot_nt(q, kl_ref[hd, c * ATTN_TK:(c + 1) * ATTN_TK, :]))
                va.append(vl_ref[hd, c * ATTN_TK:(c + 1) * ATTN_TK, :])
            scores.append(sc)
            values.append(va)
        for hd in heads:
            o_ref[:, hd * MLA_V:(hd + 1) * MLA_V] = _softmax_pv(scores[hd], values[hd]).astype(o_ref.dtype)

    @pl.when(qi >= nq_lat)
    def _():
        for hd in heads:
            o = _softmax_pv([_dot_nt(q_ref[hd], kc_ref[hd])], [vc_ref[hd]])
            o_ref[:, hd * MLA_V:(hd + 1) * MLA_V] = o.astype(o_ref.dtype)


def _attention(qh, kh, vh, n_batch, t_len, ctx_len, with_ctx):
    H, ntot, _ = qh.shape
    n_lat = n_batch * t_len
    tq = ATTN_TQ
    hp = ATTN_HEADS_PER_STEP
    assert ctx_len == tq and t_len % ATTN_TK == 0 and H % hp == 0
    nq_lat = t_len // tq
    nq = nq_lat + (1 if with_ctx else 0)
    ctx_blk0 = n_lat // ctx_len

    def qrow(b, qi):
        return jnp.where(qi < nq_lat, b * nq_lat + qi, n_lat // tq + b)

    kern = functools.partial(_attn_kernel, nq_lat=nq_lat, n_chunks=t_len // ATTN_TK)
    return pl.pallas_call(
        kern,
        grid=(n_batch, H // hp, nq),
        in_specs=[pl.BlockSpec((hp, tq, MLA_QK_PAD), lambda b, h, qi: (h, qrow(b, qi), 0)),
                  pl.BlockSpec((hp, t_len, MLA_QK_PAD), lambda b, h, qi: (h, b, 0)),
                  pl.BlockSpec((hp, ctx_len, MLA_QK_PAD), lambda b, h, qi: (h, ctx_blk0 + b, 0)),
                  pl.BlockSpec((hp, t_len, MLA_V), lambda b, h, qi: (h, b, 0)),
                  pl.BlockSpec((hp, ctx_len, MLA_V), lambda b, h, qi: (h, ctx_blk0 + b, 0))],
        out_specs=pl.BlockSpec((tq, hp * MLA_V), lambda b, h, qi: (qrow(b, qi), h)),
        out_shape=jax.ShapeDtypeStruct((ntot, H * MLA_V), BF16),
        compiler_params=_params("parallel", "parallel", "arbitrary"),
        name="mla_attention",
    )(qh, kh, kh, vh, vh)


def _tri(n, reverse, transpose=False):
    i = lax.broadcasted_iota(jnp.int32, (n, n), 0)
    j = lax.broadcasted_iota(jnp.int32, (n, n), 1)
    if transpose:
        i, j = j, i
    return ((j >= i) if reverse else (j <= i)).astype(F32)


def _interleave(gens, skew):
    out = [None] * len(gens)
    live = set(range(len(gens)))
    tick = 0
    while live:
        for c in sorted(live):
            if tick < c * skew:
                continue
            try:
                next(gens[c])
            except StopIteration as stop:
                out[c] = stop.value
                live.discard(c)
        tick += 1
    return out


def _tri_inverse_consts(n, size):
    sub = SUBLANES
    ii = lax.broadcasted_iota(jnp.int32, (n, n), 0)
    jj = lax.broadcasted_iota(jnp.int32, (n, n), 1)
    diag = (ii // sub) == (jj // sub)
    lane = lax.broadcasted_iota(jnp.int32, (sub, n), 1) % sub
    row = lax.broadcasted_iota(jnp.int32, (sub, n), 0)
    offs = []
    s = sub
    while s < size:
        offs.append(((ii // (2 * s)) == (jj // (2 * s))) & ((ii // s) != (jj // s)))
        s *= 2
    return diag, lane, row, offs


def _diag_block_inverse(nmat, rev, consts):
    diag, lane, row, _ = consts
    n = nmat.shape[0]
    sub = SUBLANES
    nc = jnp.sum(jnp.where(diag, nmat, 0.0).reshape(n // sub, sub, n), axis=0)
    cols = []
    for j in range(sub):
        one = jnp.where(lane == j, nc, 0.0)
        up, dn = one, one
        for sh in (1, 2, 4):
            up = up + pltpu.roll(up, sh, 1)
            dn = dn + pltpu.roll(dn, n - sh, 1)
        cols.append(jnp.where(lane >= j, up, dn))
    x = (lane == row).astype(F32)
    for j in (range(sub - 1, 0, -1) if rev else range(sub - 1)):
        x = x + cols[j] * x[j:j + 1, :]
    return jnp.where(diag, jnp.broadcast_to(x[None], (n // sub, sub, n)).reshape(n, n), 0.0)


def _unit_tri_inverse_steps(nmat, rev, consts):
    x = _diag_block_inverse(nmat, rev, consts)
    yield
    for off in consts[3]:
        xb = x.astype(BF16)
        t1 = _dot(xb, jnp.where(off, nmat, 0.0).astype(BF16)).astype(BF16)
        yield
        x = x + _dot(t1, xb)
        yield
    return x


def _unit_tri_inverses(nmats, revs, size):
    consts = _tri_inverse_consts(nmats[0].shape[0], size)
    return _interleave([_unit_tri_inverse_steps(nm, rev, consts) for nm, rev in zip(nmats, revs)], CHAIN_SKEW)


def _rwkv_chunks(chains, revs):
    C = RWKV_CHUNK
    n2 = 2 * C
    tri = {rev: _tri(C, rev).astype(BF16) for rev in set(revs)}
    lane = lax.broadcasted_iota(jnp.int32, (C, LANES), 1)
    m0 = (lane < RWKV_HEAD).astype(F32)
    m1 = 1.0 - m0
    ii = lax.broadcasted_iota(jnp.int32, (n2, n2), 0)
    jj = lax.broadcasted_iota(jnp.int32, (n2, n2), 1)
    same = (ii // C) == (jj // C)
    il, jl = ii % C, jj % C
    strict = {rev: same & ((jl > il) if rev else (jl < il)) for rev in set(revs)}
    incl = {rev: same & ((jl >= il) if rev else (jl <= il)) for rev in set(revs)}
    inv_consts = _tri_inverse_consts(n2, C)

    def ex(x):
        return jnp.concatenate([x * m0, x * m1], axis=0)

    def chain(r, v, a, lw, kd, bd, rev):
        cum = _dot_split3(tri[rev], lw)
        tot = jnp.sum(lw, axis=0, keepdims=True)
        yield
        e_neg = jnp.exp(-cum)
        e_rem = jnp.exp(tot - cum)
        at, rt = ex(a * jnp.exp(cum - lw)), ex(r * jnp.exp(cum))
        bt, kt = ex(bd * e_neg), ex(kd * e_neg)
        bh, kh = ex(bd * e_rem).astype(BF16), ex(kd * e_rem).astype(BF16)
        ve = ex(v).astype(BF16)
        yield
        aa = _dot_nt(jnp.concatenate([at, rt], axis=0).astype(BF16), jnp.concatenate([bt, kt], axis=0).astype(BF16))
        yield
        a_ab = jnp.where(strict[rev], aa[:n2, :n2], 0.0)
        a_ak = jnp.where(strict[rev], aa[:n2, n2:], 0.0).astype(BF16)
        a_rb = jnp.where(incl[rev], aa[n2:, :n2], 0.0).astype(BF16)
        a_rk = jnp.where(incl[rev], aa[n2:, n2:], 0.0).astype(BF16)
        akv = _dot(a_ak, ve)
        y0b = _dot(a_rk, ve)
        yield
        tinv = yield from _unit_tri_inverse_steps(a_ab, rev, inv_consts)
        zb = _dot(tinv.astype(BF16), jnp.concatenate([at, akv], axis=1).astype(BF16)).astype(BF16)
        yield
        w2 = _dot(a_rb, zb)
        gmt = _dot_tn(zb[:, :LANES], bh).astype(BF16)
        s0t = _dot_tn(jnp.concatenate([zb[:, LANES:], ve], axis=0), jnp.concatenate([bh, kh], axis=0))
        yield
        rhat = (rt + w2[:, :LANES]).astype(BF16)
        y0 = w2[:, LANES:] + y0b
        return rhat, y0, gmt, s0t, jnp.exp(tot)

    return _interleave([chain(*c, rev) for c, rev in zip(chains, revs)], CHAIN_SKEW)


def _rwkv_apply(pre, states):
    C = RWKV_CHUNK
    sbs = [s.astype(BF16) for s in states]
    yes = [_dot_nt(rh, sb) + y0 for (rh, y0, _, _, _), sb in zip(pre, sbs)]
    ys = [ye[:C] + ye[C:] for ye in yes]
    new_states = [s * dec + _dot(sb, g) + s0 for (_, _, g, s0, dec), s, sb in zip(pre, states, sbs)]
    return ys, new_states


def _rwkv_scan_kernel(rf, rb, vf, vb, af, ab, lw0, lw1, kd0, kd1, b0, b1, yf_ref, yb_ref, s_ref):
    @pl.when(pl.program_id(1) == 0)
    def _():
        s_ref[...] = jnp.zeros_like(s_ref)

    C = RWKV_CHUNK
    nsub = rf.shape[0] // C
    dirs = ((rf, vf, af, lw0, kd0, b0, yf_ref), (rb, vb, ab, lw1, kd1, b1, yb_ref))
    npair = RWKV_WIDTH // LANES
    chains, revs, where = [], [], []
    for d, (r_, v_, a_, lw_, kd_, b_, y_) in enumerate(dirs):
        for j in range(npair):
            sl = slice(j * LANES, (j + 1) * LANES)
            for c in range(nsub):
                rows = slice(c * C, (c + 1) * C)
                chains.append(tuple(ref[rows, sl] for ref in (r_, v_, a_, lw_, kd_, b_)))
                revs.append(d == 1)
                where.append((d, j, c))
    pre = dict(zip(where, _rwkv_chunks(chains, revs)))
    keys = [(d, j) for d in range(2) for j in range(npair)]
    states = [s_ref[d, j] for d, j in keys]
    for step in range(nsub):
        sub = [step if d == 0 else nsub - 1 - step for d, _ in keys]
        ys, states = _rwkv_apply([pre[(d, j, c)] for (d, j), c in zip(keys, sub)], states)
        for (d, j), c, y in zip(keys, sub, ys):
            dirs[d][6][c * C:(c + 1) * C, j * LANES:(j + 1) * LANES] = y
    for (d, j), s_new in zip(keys, states):
        s_ref[d, j] = s_new


def _chunk_maps(n_batch, t_len, ctx_len, chunk):
    nctx, nlat = ctx_len // chunk, t_len // chunk
    base = n_batch * nlat

    def fwd(b, i):
        return jnp.where(i < nctx, base + b * nctx + i, b * nlat + i - nctx)

    def bwd(b, i):
        return jnp.where(i < nctx, base + b * nctx + (nctx - 1 - i), b * nlat + (nlat - 1 - (i - nctx)))

    return fwd, bwd, nctx + nlat


def _rwkv_scan(r, v, a, lw0, lw1, kd0, kd1, b0, b1, n_batch, t_len, ctx_len):
    ntot, W = r.shape
    C = RWKV_BLOCK
    fwd, bwd, nsteps = _chunk_maps(n_batch, t_len, ctx_len, C)
    fs = pl.BlockSpec((C, W), lambda b, i: (fwd(b, i), 0))
    bs = pl.BlockSpec((C, W), lambda b, i: (bwd(b, i), 0))
    return pl.pallas_call(
        _rwkv_scan_kernel,
        grid=(n_batch, nsteps),
        in_specs=[fs, bs, fs, bs, fs, bs, fs, bs, fs, bs, fs, bs],
        out_specs=[fs, bs],
        out_shape=[jax.ShapeDtypeStruct((ntot, W), F32)] * 2,
        scratch_shapes=[pltpu.VMEM((2, W // LANES, LANES, LANES), F32)],
        compiler_params=_params("parallel", "arbitrary"),
        name="rwkv7_scan",
    )(r, r, v, v, a, a, lw0, lw1, kd0, kd1, b0, b1)


def _gdn_scan_kernel(qf, qb, kf, kb, vf, vb, gf, gb, of_ref, ob_ref, s_ref):
    @pl.when(pl.program_id(1) == 0)
    def _():
        s_ref[...] = jnp.zeros_like(s_ref)

    C = GDN_CHUNK
    DH = GDN_HEAD
    ii = lax.broadcasted_iota(jnp.int32, (C, C), 0)
    jj = lax.broadcasted_iota(jnp.int32, (C, C), 1)
    strict = {False: jj < ii, True: jj > ii}
    incl = {False: jj <= ii, True: jj >= ii}
    nsub = qf.shape[0] // C
    dirs = ((qf, kf, vf, gf, of_ref), (qb, kb, vb, gb, ob_ref))
    qs, ks, vs, gcs, grs, gts, betas, revs, where = [], [], [], [], [], [], [], [], []
    for d, (q_, k_, v_, g_, o_) in enumerate(dirs):
        rev = d == 1
        for c in range(nsub):
            rows = slice(c * C, (c + 1) * C)
            gbv = g_[rows, :]
            gcol = jnp.dot(_tri(C, rev), gbv, precision=HIGHEST, preferred_element_type=F32)
            grow = jnp.dot(gbv.T, _tri(C, rev, transpose=True), precision=HIGHEST, preferred_element_type=F32)
            gtot = jnp.sum(gbv, axis=0, keepdims=True)
            for hd in range(GDN_HEADS):
                sl = slice(hd * DH, (hd + 1) * DH)
                cg = d * GDN_HEADS + hd
                cb = 2 * GDN_HEADS + cg
                qs.append(q_[rows, sl]); ks.append(k_[rows, sl]); vs.append(v_[rows, sl])
                gcs.append(gcol[:, cg:cg + 1]); grs.append(grow[cg:cg + 1, :]); gts.append(gtot[:, cg:cg + 1])
                betas.append(gbv[:, cb:cb + 1]); revs.append(rev); where.append((d, hd, c))

    decs = [jnp.exp(jnp.where(incl[rev], gc - gr, -1e30)) for gc, gr, rev in zip(gcs, grs, revs)]
    kbetas = [k * b for k, b in zip(ks, betas)]
    qks = [_dot_nt(jnp.concatenate([kb_, q], axis=0).astype(BF16), k.astype(BF16)) for kb_, q, k in zip(kbetas, qs, ks)]
    nmats = [jnp.where(strict[rev], -qk[:C] * dec, 0.0) for qk, dec, rev in zip(qks, decs, revs)]
    aints = [jnp.where(incl[rev], qk[C:] * dec, 0.0).astype(BF16) for qk, dec, rev in zip(qks, decs, revs)]
    egs = [jnp.exp(gc) for gc in gcs]
    rhss = [jnp.concatenate([v * b, kb_ * eg], axis=1).astype(BF16) for v, b, kb_, eg in zip(vs, betas, kbetas, egs)]
    khats = [(k * jnp.exp(gt - gc)).astype(BF16) for k, gt, gc in zip(ks, gts, gcs)]
    tinvs = _unit_tri_inverses(nmats, revs, C)
    sols = [_dot(ti.astype(BF16), rhs).astype(BF16) for ti, rhs in zip(tinvs, rhss)]
    qos = [_dot(ai, sol) for ai, sol in zip(aints, sols)]
    qhats = [(q * eg - qo[:, DH:]).astype(BF16) for q, eg, qo in zip(qs, egs, qos)]
    gss = [_dot_tn(kh, sol) for kh, sol in zip(khats, sols)]
    pre = dict(zip(where, zip(qhats, qos, gss, [jnp.exp(gt) for gt in gts])))

    keys = [(d, hd) for d in range(2) for hd in range(GDN_HEADS)]
    states = [s_ref[d, hd] for d, hd in keys]
    for step in range(nsub):
        sub = [step if d == 0 else nsub - 1 - step for d, _ in keys]
        cur = [pre[(d, hd, c)] for (d, hd), c in zip(keys, sub)]
        sbs = [s.astype(BF16) for s in states]
        os_ = [_dot(qh, sb) + qo[:, :DH] for (qh, qo, _, _), sb in zip(cur, sbs)]
        states = [s * dec - _dot(gs[:, DH:].astype(BF16), sb) + gs[:, :DH]
                  for (_, _, gs, dec), s, sb in zip(cur, states, sbs)]
        for (d, hd), c, o in zip(keys, sub, os_):
            dirs[d][4][c * C:(c + 1) * C, hd * DH:(hd + 1) * DH] = o
    for (d, hd), s_new in zip(keys, states):
        s_ref[d, hd] = s_new


def _gdn_scan(q, k, v, gb, n_batch, t_len, ctx_len):
    ntot, W = q.shape
    C = GDN_BLOCK
    fwd, bwd, nsteps = _chunk_maps(n_batch, t_len, ctx_len, C)
    fs = pl.BlockSpec((C, W), lambda b, i: (fwd(b, i), 0))
    bs = pl.BlockSpec((C, W), lambda b, i: (bwd(b, i), 0))
    gfs = pl.BlockSpec((C, LANES), lambda b, i: (fwd(b, i), 0))
    gbs = pl.BlockSpec((C, LANES), lambda b, i: (bwd(b, i), 0))
    return pl.pallas_call(
        _gdn_scan_kernel,
        grid=(n_batch, nsteps),
        in_specs=[fs, bs, fs, bs, fs, bs, gfs, gbs],
        out_specs=[fs, bs],
        out_shape=[jax.ShapeDtypeStruct((ntot, W), F32)] * 2,
        scratch_shapes=[pltpu.VMEM((2, GDN_HEADS, GDN_HEAD, GDN_HEAD), F32)],
        compiler_params=_params("parallel", "arbitrary"),
        name="gdn_scan",
    )(q, q, k, k, v, v, gb, gb)


def _outproj_kernel(x_ref, a_ref, yf_ref, yb_ref, r_ref, v_ref, kd0_ref, kd1_ref, g_ref, of_ref, ob_ref, z_ref,
                    rk_ref, gng_ref, gnb_ref, gdn_ref, seg_ref, w_ref, mod_ref, lng_ref, lnb_ref, o_ref, *, alpha):
    seg = seg_ref[...]
    y = yf_ref[...] + yb_ref[...]
    yc = y - _seg_sum(y, seg) * (1.0 / RWKV_HEAD)
    var = _seg_sum(yc * yc, seg) * (1.0 / RWKV_HEAD)
    yn = yc * lax.rsqrt(var + RWKV_GN_EPS) * gng_ref[...] + gnb_ref[...]
    bonus = _seg_sum(r_ref[...] * (kd0_ref[...] + kd1_ref[...]) * rk_ref[...], seg) * v_ref[...]
    b_mix = ((yn + bonus) * g_ref[...]).astype(BF16)
    o = of_ref[...] + ob_ref[...]
    z = z_ref[...]
    parts = []
    for hd in range(GDN_HEADS):
        oh = o[:, hd * GDN_HEAD:(hd + 1) * GDN_HEAD]
        parts.append(oh * lax.rsqrt(jnp.mean(oh * oh, axis=-1, keepdims=True) + 1e-6))
    c_mix = (jnp.concatenate(parts, axis=1) * gdn_ref[...] * (z * jax.nn.sigmoid(z))).astype(BF16)

    na, nb = a_ref.shape[1], RWKV_WIDTH
    acc = _dot(a_ref[...], w_ref[0:na, :])
    acc += _dot(b_mix, w_ref[na:na + nb, :])
    acc += _dot(c_mix, w_ref[na + nb:, :])
    gate = mod_ref[0][2:3]
    o_ref[...] = _layer_norm(alpha * x_ref[...] + gate * acc, lng_ref[...], lnb_ref[...])


def _out_proj(h, a, rw, gd, p, r_k, gn_g, gn_b, gdn_norm, seg, w, mod, ln_g, ln_b,
              n_rows, n_lat, t_len, n_batch, tm, alpha):
    D = h.shape[1]
    W = RWKV_WIDTH

    def mod_row(i):
        return jnp.where(i * tm < n_lat, (i * tm) // t_len, n_batch)

    tok = pl.BlockSpec((tm, W), lambda i: (i, 0))
    vec = pl.BlockSpec((1, W), lambda i: (0, 0))
    return pl.pallas_call(
        functools.partial(_outproj_kernel, alpha=alpha),
        grid=(n_rows // tm,),
        in_specs=[pl.BlockSpec((tm, D), lambda i: (i, 0)),
                  pl.BlockSpec((tm, a.shape[1]), lambda i: (i, 0))]
                 + [tok] * 9
                 + [pl.BlockSpec((tm, W), lambda i: (i, COL_Z // W)),
                    vec, vec, vec, vec,
                    pl.BlockSpec(seg.shape, lambda i: (0, 0)),
                    pl.BlockSpec(w.shape, lambda i: (0, 0)),
                    pl.BlockSpec((1, 8, D), lambda i: (mod_row(i), 0, 0)),
                    pl.BlockSpec((1, D), lambda i: (0, 0)),
                    pl.BlockSpec((1, D), lambda i: (0, 0))],
        out_specs=pl.BlockSpec((tm, D), lambda i: (i, 0)),
        out_shape=jax.ShapeDtypeStruct((n_rows, D), F32),
        compiler_params=_params("parallel"),
        name="out_proj_ln",
    )(h, a, *rw, *gd, p, r_k, gn_g, gn_b, gdn_norm, seg, w, mod, ln_g, ln_b)


def _ffn_kernel(x_ref, mod_ref, wg_ref, wu_ref, wd_ref, g_ref, beta_ref, o_ref, xm_ref, acc_ref, *, alpha):
    f = pl.program_id(1)

    @pl.when(f == 0)
    def _():
        m = mod_ref[0]
        xm_ref[...] = (x_ref[...] * (1.0 + m[4:5]) + m[3:4]).astype(BF16)
        acc_ref[...] = jnp.zeros_like(acc_ref)

    xm = xm_ref[...]
    gate = _dot(xm, wg_ref[...])
    up = _dot(xm, wu_ref[...])
    act = gate * jax.nn.sigmoid(gate) * up
    acc_ref[...] += _dot(act.astype(BF16), wd_ref[...])

    @pl.when(f == pl.num_programs(1) - 1)
    def _():
        m = mod_ref[0]
        o_ref[...] = _layer_norm(alpha * x_ref[...] + m[5:6] * acc_ref[...], g_ref[...], beta_ref[...])


def _ffn(h, mod, wg, wu, wd, ln_g, ln_b, n_rows, n_lat, t_len, n_batch, tm, tf, alpha):
    D = h.shape[1]
    F = wg.shape[1]

    def mod_row(i):
        return jnp.where(i * tm < n_lat, (i * tm) // t_len, n_batch)

    return pl.pallas_call(
        functools.partial(_ffn_kernel, alpha=alpha),
        grid=(n_rows // tm, F // tf),
        in_specs=[pl.BlockSpec((tm, D), lambda i, f: (i, 0)),
                  pl.BlockSpec((1, 8, D), lambda i, f: (mod_row(i), 0, 0)),
                  pl.BlockSpec((D, tf), lambda i, f: (0, f)),
                  pl.BlockSpec((D, tf), lambda i, f: (0, f)),
                  pl.BlockSpec((tf, D), lambda i, f: (f, 0)),
                  pl.BlockSpec((1, D), lambda i, f: (0, 0)),
                  pl.BlockSpec((1, D), lambda i, f: (0, 0))],
        out_specs=pl.BlockSpec((tm, D), lambda i, f: (i, 0)),
        out_shape=jax.ShapeDtypeStruct((n_rows, D), F32),
        scratch_shapes=[pltpu.VMEM((tm, D), BF16), pltpu.VMEM((tm, D), F32)],
        compiler_params=_params("parallel", "arbitrary"),
        name="ffn_ln",
    )(h, mod, wg, wu, wd, ln_g, ln_b)


def _in_proj_columns():
    kr0 = 2 * MLA_RANK
    swap = np.concatenate([np.arange(16, 32), np.arange(0, 16), np.arange(48, 64), np.arange(32, 48)])
    pr0 = kr0 + MLA_ROPE
    n_lora = 2 * RWKV_LORA_W + 2 * RWKV_LORA_A + RWKV_LORA_G
    pg0 = pr0 + 3 * RWKV_WIDTH + n_lora
    nh4 = 4 * GDN_HEADS
    idx = -np.ones(IN_COLS_PAD, np.int64)

    def put(dst, src):
        idx[dst:dst + len(src)] = src

    put(COL_RKV, pr0 + np.arange(3 * RWKV_WIDTH))
    put(COL_GQKV, pg0 + np.arange(3 * GDN_WIDTH))
    put(COL_Z, pg0 + 3 * GDN_WIDTH + np.arange(GDN_WIDTH))
    put(COL_CQ, np.arange(MLA_RANK))
    put(COL_CKV, MLA_RANK + np.arange(MLA_RANK))
    put(COL_LORA, pr0 + 3 * RWKV_WIDTH + np.arange(n_lora))
    put(COL_KR, kr0 + np.arange(MLA_ROPE))
    put(COL_KR + MLA_ROPE, kr0 + swap)
    put(COL_GAB, pg0 + 4 * GDN_WIDTH + np.arange(nh4))
    return idx


def _take_cols(w, idx):
    safe = jnp.asarray(np.maximum(idx, 0), jnp.int32)
    mask = jnp.asarray(idx >= 0)
    return jnp.where(mask, jnp.take(w, safe, axis=-1), 0.0)


def _uq_columns():
    swap = np.concatenate([np.arange(16, 32), np.arange(0, 16), np.arange(48, 64), np.arange(32, 48)])
    per = MLA_NOPE + MLA_ROPE
    out = []
    for hd in range(MLA_HEADS):
        out += [hd * per + np.arange(MLA_NOPE), hd * per + MLA_NOPE + np.arange(MLA_ROPE), hd * per + MLA_NOPE + swap]
    return np.concatenate(out)


def _rope_tables(t_len, tm):
    rows = t_len // GRID_W
    row = jnp.repeat(jnp.arange(rows), GRID_W)
    col = jnp.tile(jnp.arange(GRID_W), rows)
    n_freq = MLA_ROPE // 4
    inv = ROPE_THETA ** (-jnp.arange(n_freq, dtype=F32) / n_freq)
    ar = row.astype(F32)[:, None] * inv
    ac = col.astype(F32)[:, None] * inv
    cos = jnp.concatenate([jnp.cos(ar), jnp.cos(ar), jnp.cos(ac), jnp.cos(ac)], axis=1)
    sin = jnp.concatenate([-jnp.sin(ar), jnp.sin(ar), -jnp.sin(ac), jnp.sin(ac)], axis=1)
    zer = jnp.zeros((t_len, LANES - MLA_ROPE), F32)
    cs1 = jnp.concatenate([cos, zer], axis=1)
    cs2 = jnp.concatenate([sin, zer], axis=1)
    ident = jnp.concatenate([jnp.ones((tm, MLA_ROPE), F32), jnp.zeros((tm, LANES - MLA_ROPE), F32)], axis=1)
    return jnp.concatenate([cs1, ident], axis=0), jnp.concatenate([cs2, jnp.zeros((tm, LANES), F32)], axis=0)


def _fill_halo(ext_ref, cur_ref, prev_ref, next_ref, first, last):
    tm = cur_ref.shape[0]
    ext_ref[0:HALO, :] = jnp.where(first, 0.0, prev_ref[...])
    ext_ref[HALO:HALO + tm, :] = cur_ref[...]
    ext_ref[HALO + tm:, :] = jnp.where(last, 0.0, next_ref[...])


def _seq_edges(tm, n_lat, t_len, ctx_len):
    t0 = pl.program_id(0) * tm
    lat = t0 < n_lat
    first = jnp.where(lat, t0 % t_len == 0, (t0 - n_lat) % ctx_len == 0)
    last = jnp.where(lat, (t0 + tm) % t_len == 0, (t0 - n_lat + tm) % ctx_len == 0)
    return first, last


def _softplus(x):
    return jnp.maximum(x, 0.0) + jnp.log(1.0 + jnp.exp(-jnp.abs(x)))


def _rwkv_prep_kernel(rkv_ref, rkv_p, rkv_n, lg_ref, lg_p, lg_n, mu_rkv, mu_lg, w0_ref, a0_ref, w2_ref, a2_ref,
                      g2_ref, kk_ref, ka_ref, seg_ref,
                      r_o, v_o, a_o, lw0_o, lw1_o, kd0_o, kd1_o, b0_o, b1_o, g_o, ext_rkv, ext_lg,
                      *, n_lat, t_len, ctx_len):
    tm = rkv_ref.shape[0]
    W = RWKV_WIDTH
    first, last = _seq_edges(tm, n_lat, t_len, ctx_len)
    _fill_halo(ext_rkv, rkv_ref, rkv_p, rkv_n, first, last)
    _fill_halo(ext_lg, lg_ref, lg_p, lg_n, first, last)

    def mixed(ext, mu):
        cur = ext[HALO:HALO + tm, :]
        sh = 0.5 * (ext[HALO - 1:HALO - 1 + tm, :] + ext[HALO + 1:HALO + 1 + tm, :])
        return cur + (sh - cur) * mu[...]

    p = mixed(ext_rkv, mu_rkv)
    lg = mixed(ext_lg, mu_lg)
    r, k, v = p[:, :W], p[:, W:2 * W], p[:, 2 * W:]
    lora = lg[:, :LANES]
    tw = jnp.tanh(lora).astype(BF16)
    ab = lora.astype(BF16)
    kkr = k * kk_ref[...]
    kk = kkr * lax.rsqrt(_seg_sum(kkr * kkr, seg_ref[...]) + 1e-12)
    r_o[...] = r
    v_o[...] = v
    a_o[...] = -kk
    g_o[...] = _dot(jax.nn.sigmoid(lg[:, LANES:]).astype(BF16), g2_ref[...])
    for d, (lw_o, kd_o, b_o) in enumerate(((lw0_o, kd0_o, b0_o), (lw1_o, kd1_o, b1_o))):
        w_log = -_softplus(-(w0_ref[d:d + 1, :] + _dot(tw, w2_ref[d]))) - 0.5
        lw_o[...] = -jnp.exp(w_log)
        a_lr = jax.nn.sigmoid(a0_ref[d:d + 1, :] + _dot(ab, a2_ref[d]))
        kd_o[...] = k * (1.0 + (a_lr - 1.0) * ka_ref[...])
        b_o[...] = kk * a_lr


def _halo_specs(tm, width, col, ntot):
    nb = tm // HALO
    last_blk = ntot // HALO - 1
    return [pl.BlockSpec((tm, width), lambda i: (i, col)),
            pl.BlockSpec((HALO, width), lambda i: (jnp.maximum(i * nb - 1, 0), col)),
            pl.BlockSpec((HALO, width), lambda i: (jnp.minimum((i + 1) * nb, last_blk), col))]


def _rwkv_prep(p, mu_rkv, mu_lg, w0, a0, w2p, a2p, g2p, k_k, k_a, seg, n_lat, t_len, ctx_len, tm):
    ntot = p.shape[0]
    W = RWKV_WIDTH
    full = lambda arr: pl.BlockSpec(arr.shape, lambda i: (0,) * arr.ndim)
    params = (mu_rkv, mu_lg, w0, a0, w2p, a2p, g2p, k_k, k_a, seg)
    return pl.pallas_call(
        functools.partial(_rwkv_prep_kernel, n_lat=n_lat, t_len=t_len, ctx_len=ctx_len),
        grid=(ntot // tm,),
        in_specs=_halo_specs(tm, 3 * W, COL_RKV // (3 * W), ntot) + _halo_specs(tm, 2 * LANES, COL_LORA // (2 * LANES), ntot)
                 + [full(a) for a in params],
        out_specs=[pl.BlockSpec((tm, W), lambda i: (i, 0))] * 10,
        out_shape=[jax.ShapeDtypeStruct((ntot, W), F32)] * 10,
        scratch_shapes=[pltpu.VMEM((tm + 2 * HALO, 3 * W), F32), pltpu.VMEM((tm + 2 * HALO, 2 * LANES), F32)],
        compiler_params=_params("parallel"),
        name="rwkv7_prep",
    )(p, p, p, p, p, p, *params)


def _gdn_prep_kernel(x_ref, x_p, x_n, ab_ref, cw_ref, alog_ref, dt_ref, q_o, k_o, v_o, gb_o, ext,
                     *, n_lat, t_len, ctx_len):
    tm = x_ref.shape[0]
    W = GDN_WIDTH
    first, last = _seq_edges(tm, n_lat, t_len, ctx_len)
    _fill_halo(ext, x_ref, x_p, x_n, first, last)
    half = GDN_CONV // 2
    conv = functools.reduce(jnp.add, [ext[HALO - half + j:HALO - half + j + tm, :] * cw_ref[j:j + 1, :]
                                      for j in range(GDN_CONV)])
    act = conv * jax.nn.sigmoid(conv)

    def l2n(x):
        parts = []
        for hd in range(GDN_HEADS):
            xh = x[:, hd * GDN_HEAD:(hd + 1) * GDN_HEAD]
            parts.append(xh * lax.rsqrt(jnp.sum(xh * xh, axis=-1, keepdims=True) + 1e-12))
        return jnp.concatenate(parts, axis=1)

    q_o[...] = l2n(act[:, :W]) * GDN_HEAD ** -0.5
    k_o[...] = l2n(act[:, W:2 * W])
    v_o[...] = act[:, 2 * W:]
    ab = ab_ref[...]
    lane = lax.broadcasted_iota(jnp.int32, ab.shape, 1)
    nh2 = 2 * GDN_HEADS
    glog = -jnp.exp(alog_ref[...]) * _softplus(ab + dt_ref[...])
    gb_o[...] = jnp.where(lane < nh2, glog, jnp.where(lane < 2 * nh2, jax.nn.sigmoid(ab), 0.0))


def _gdn_prep(p, conv_w, a_log, dt_bias, n_lat, t_len, ctx_len, tm):
    ntot = p.shape[0]
    W = GDN_WIDTH
    full = lambda arr: pl.BlockSpec(arr.shape, lambda i: (0,) * arr.ndim)
    return pl.pallas_call(
        functools.partial(_gdn_prep_kernel, n_lat=n_lat, t_len=t_len, ctx_len=ctx_len),
        grid=(ntot // tm,),
        in_specs=_halo_specs(tm, 3 * W, COL_GQKV // (3 * W), ntot)
                 + [pl.BlockSpec((tm, LANES), lambda i: (i, COL_GAB // LANES)), full(conv_w), full(a_log), full(dt_bias)],
        out_specs=[pl.BlockSpec((tm, W), lambda i: (i, 0))] * 3 + [pl.BlockSpec((tm, LANES), lambda i: (i, 0))],
        out_shape=[jax.ShapeDtypeStruct((ntot, W), F32)] * 3 + [jax.ShapeDtypeStruct((ntot, LANES), F32)],
        scratch_shapes=[pltpu.VMEM((tm + 2 * HALO, 3 * W), F32)],
        compiler_params=_params("parallel"),
        name="gdn_prep",
    )(p, p, p, p, conv_w, a_log, dt_bias)


def kernel(x, c, ctx, c_ctx, w_mod, b_mod, w_in, mla_q_norm, mla_kv_norm, mla_w_uq, mla_w_ukv, rwkv_mu, rwkv_w0, rwkv_w2, rwkv_a0, rwkv_a2, rwkv_g2, rwkv_k_k, rwkv_k_a, rwkv_r_k, rwkv_gn_g, rwkv_gn_b, gdn_conv, gdn_a_log, gdn_dt_bias, gdn_norm, w_out, ln1_g, ln1_b, ffn_w_gate, ffn_w_up, ffn_w_down, ln2_g, ln2_b):
    B, T, D = x.shape
    CTX = ctx.shape[1]
    L = w_mod.shape[0]
    n_lat, n_ctx = B * T, B * CTX
    ntot = n_lat + n_ctx
    alpha = (2.0 * L) ** 0.25
    tm = 256
    tm_big = 512 if (n_lat % 512 == 0 and n_ctx % 512 == 0 and T % 512 == 0) else tm
    tf = 512
    assert T % tm == 0 and n_ctx % tm == 0 and B + 1 <= 8

    cvec = jnp.concatenate([c, c_ctx[None], jnp.zeros((8 - B - 1, D), F32)], axis=0)
    mod_all = _modulation_all(cvec, w_mod, b_mod).reshape(L, 8, 6, D)[:, :B + 1]
    mod_all = jnp.pad(mod_all, ((0, 0), (0, 0), (0, 2), (0, 0)))

    in_idx = _in_proj_columns()
    w_in_p = _take_cols(w_in, in_idx).astype(BF16)
    uq_idx = _uq_columns()
    w_uq_p = jnp.take(mla_w_uq, jnp.asarray(uq_idx, jnp.int32), axis=-1).astype(BF16)
    w_ukv_b = mla_w_ukv.astype(BF16)
    w_out_b = w_out.astype(BF16)
    wg_b, wu_b, wd_b = ffn_w_gate.astype(BF16), ffn_w_up.astype(BF16), ffn_w_down.astype(BF16)
    cs1, cs2 = _rope_tables(T, tm)

    W = RWKV_WIDTH
    mu_rkv = rwkv_mu[:, None, :3 * W]
    mu_lg = jnp.pad(rwkv_mu[:, None, 3 * W:], ((0, 0), (0, 0), (0, 2 * LANES - (rwkv_mu.shape[1] - 3 * W))))
    lw, la = RWKV_LORA_W, RWKV_LORA_A
    w2p = jnp.zeros((L, 2, LANES, W), F32)
    a2p = jnp.zeros((L, 2, LANES, W), F32)
    for d in range(2):
        w2p = w2p.at[:, d, d * lw:(d + 1) * lw].set(rwkv_w2[:, d])
        a2p = a2p.at[:, d, 2 * lw + d * la:2 * lw + (d + 1) * la].set(rwkv_a2[:, d])
    w2p, a2p = w2p.astype(BF16), a2p.astype(BF16)
    g2p = jnp.pad(rwkv_g2, ((0, 0), (0, LANES - RWKV_LORA_G), (0, 0))).astype(BF16)
    lane_head = np.arange(W) // RWKV_HEAD
    seg = jnp.asarray(lane_head[:, None] == lane_head[None, :], BF16)
    conv_p = jnp.pad(gdn_conv, ((0, 0), (0, 8 - GDN_CONV), (0, 0)))
    nh2 = 2 * GDN_HEADS
    alog_p = jnp.pad(gdn_a_log.reshape(L, 1, nh2), ((0, 0), (0, 0), (0, LANES - nh2)))
    dt_p = jnp.pad(gdn_dt_bias.reshape(L, 1, nh2), ((0, 0), (0, 0), (0, LANES - nh2)))
    r_k = rwkv_r_k.reshape(L, 1, W)
    gdn_norm_t = jnp.tile(gdn_norm, (1, GDN_HEADS))[:, None, :]

    h = jnp.concatenate([x.reshape(n_lat, D), ctx.reshape(n_ctx, D)], axis=0)
    for l in range(L):
        last = l == L - 1
        mod = mod_all[l]
        p = _in_proj(h, mod, w_in_p[l], n_lat, T, B, tm_big)

        qh, kh, vh = _mla_proj(p, mla_q_norm[l][None], mla_kv_norm[l][None], w_uq_p[l], w_ukv_b[l], cs1, cs2,
                               n_lat, T, tm)
        a_mix = _attention(qh, kh, vh, B, T, CTX, with_ctx=not last)

        r, v, a, lw0, lw1, kd0, kd1, b0, b1, g = _rwkv_prep(
            p, mu_rkv[l], mu_lg[l], rwkv_w0[l], rwkv_a0[l], w2p[l], a2p[l], g2p[l],
            rwkv_k_k[l][None], rwkv_k_a[l][None], seg, n_lat, T, CTX, tm)
        yf, yb = _rwkv_scan(r, v, a, lw0, lw1, kd0, kd1, b0, b1, B, T, CTX)

        gq, gk, gv, gb = _gdn_prep(p, conv_p[l], alog_p[l], dt_p[l], n_lat, T, CTX, tm)
        of, ob = _gdn_scan(gq, gk, gv, gb, B, T, CTX)

        n_rows = n_lat if last else ntot
        h1 = _out_proj(h, a_mix, (yf, yb, r, v, kd0, kd1, g), (of, ob), p, r_k[l], rwkv_gn_g[l][None],
                       rwkv_gn_b[l][None], gdn_norm_t[l], seg, w_out_b[l], mod, ln1_g[l][None], ln1_b[l][None],
                       n_rows, n_lat, T, B, tm, alpha)
        h = _ffn(h1, mod, wg_b[l], wu_b[l], wd_b[l], ln2_g[l][None], ln2_b[l][None],
                 n_rows, n_lat, T, B, tm_big, tf, alpha)
    return h[:n_lat].reshape(B, T, D)
```

```python
import functools
import math

import numpy as np
import jax
import jax.numpy as jnp
from jax import lax
from jax.experimental import pallas as pl
from jax.experimental.pallas import tpu as pltpu

F32 = jnp.float32
BF16 = jnp.bfloat16
HIGHEST = lax.Precision.HIGHEST

LANES = 128
SUBLANES = 8
GRID_W = 64
ROPE_THETA = 10000.0

MLA_HEADS = 8
MLA_RANK = 512
MLA_NOPE = 128
MLA_ROPE = 64
MLA_V = 128
MLA_QK_PAD = 256
ATTN_SCALE = (MLA_NOPE + MLA_ROPE) ** -0.5
Q_SCALE = ATTN_SCALE * math.log2(math.e)
ATTN_TQ = 256
ATTN_TK = 512
ATTN_HEADS_PER_STEP = 4

RWKV_HEADS = 8
RWKV_HEAD = 64
RWKV_WIDTH = RWKV_HEADS * RWKV_HEAD
RWKV_LORA_W = 32
RWKV_LORA_A = 32
RWKV_LORA_G = 96
RWKV_GN_EPS = 64e-5
RWKV_CHUNK = 64
RWKV_BLOCK = 128
CHAIN_SKEW = 0

GDN_HEADS = 4
GDN_HEAD = 128
GDN_WIDTH = GDN_HEADS * GDN_HEAD
GDN_CONV = 5
GDN_CHUNK = 128
GDN_BLOCK = 256

COL_RKV = 0
COL_GQKV = 1536
COL_Z = 3072
COL_CQ = 3584
COL_CKV = 4096
COL_LORA = 4608
COL_KR = 4864
COL_GAB = 4992
IN_COLS_PAD = 5120
HALO = 8
VMEM_LIMIT = 56 * 1024 * 1024


def _dot(a, b):
    return jnp.dot(a, b, preferred_element_type=F32)


def _dot_nt(a, b):
    return lax.dot_general(a, b, (((1,), (1,)), ((), ())), preferred_element_type=F32)


def _dot_tn(a, b):
    return lax.dot_general(a, b, (((0,), (0,)), ((), ())), preferred_element_type=F32)


def _seg_sum(x, seg):
    return _dot(x.astype(BF16), seg)


def _dot_split3(ones_bf16, x):
    t1 = x.astype(BF16)
    r1 = x - t1.astype(F32)
    t2 = r1.astype(BF16)
    t3 = (r1 - t2.astype(F32)).astype(BF16)
    return _dot(ones_bf16, t1) + _dot(ones_bf16, t2) + _dot(ones_bf16, t3)


def _layer_norm(z, g, b):
    mu = jnp.mean(z, axis=-1, keepdims=True)
    zc = z - mu
    var = jnp.mean(zc * zc, axis=-1, keepdims=True)
    return zc * lax.rsqrt(var + 1e-5) * g + b


def _params(*sem):
    return pltpu.CompilerParams(dimension_semantics=sem, vmem_limit_bytes=VMEM_LIMIT)


def _mod_kernel(c_ref, w_ref, b_ref, o_ref):
    cv = c_ref[...]
    s = cv * jax.nn.sigmoid(cv)
    o_ref[0] = _dot(s.astype(BF16), w_ref[0].astype(BF16)) + b_ref[0]


def _modulation_all(cvec, w_mod, b_mod):
    L, D, N = w_mod.shape
    tn = 1024 if N % 1024 == 0 else 512
    return pl.pallas_call(
        _mod_kernel,
        grid=(L, N // tn),
        in_specs=[pl.BlockSpec((8, D), lambda l, j: (0, 0)),
                  pl.BlockSpec((1, D, tn), lambda l, j: (l, 0, j)),
                  pl.BlockSpec((1, 1, tn), lambda l, j: (l, 0, j))],
        out_specs=pl.BlockSpec((1, 8, tn), lambda l, j: (l, 0, j)),
        out_shape=jax.ShapeDtypeStruct((L, 8, N), F32),
        compiler_params=_params("parallel", "parallel"),
        name="adaln_modulation",
    )(cvec, w_mod, b_mod.reshape(L, 1, N))


def _stream_specs(hs, tm, n_lat, row_tile):
    D = hs[0].shape[1]
    if len(hs) == 1:
        return [pl.BlockSpec((tm, D), lambda *g: (row_tile(*g), 0))]
    nl = n_lat // tm
    return [pl.BlockSpec((tm, D), lambda *g: (jnp.minimum(row_tile(*g), nl - 1), 0)),
            pl.BlockSpec((tm, D), lambda *g: (jnp.maximum(row_tile(*g) - nl, 0), 0))]


def _load_stream(x_refs, i, n_lat_tiles):
    if len(x_refs) == 1:
        return x_refs[0][...]
    return jnp.where(i < n_lat_tiles, x_refs[0][...], x_refs[1][...])


def _inproj_kernel(*refs, n_src, n_lat_tiles):
    mod_ref, w_ref, o_ref = refs[n_src:]
    m = mod_ref[0]
    xm = _load_stream(refs[:n_src], pl.program_id(1), n_lat_tiles) * (1.0 + m[1:2]) + m[0:1]
    o_ref[...] = _dot(xm.astype(BF16), w_ref[...])


def _in_proj(hs, mod, w, n_lat, t_len, n_batch, tm):
    ntot = sum(h.shape[0] for h in hs)
    D = hs[0].shape[1]
    ncol = w.shape[1]
    tn = ncol // 2

    def mod_row(i):
        return jnp.where(i * tm < n_lat, (i * tm) // t_len, n_batch)

    return pl.pallas_call(
        functools.partial(_inproj_kernel, n_src=len(hs), n_lat_tiles=n_lat // tm),
        grid=(ncol // tn, ntot // tm),
        in_specs=_stream_specs(hs, tm, n_lat, lambda j, i: i)
                 + [pl.BlockSpec((1, 8, D), lambda j, i: (mod_row(i), 0, 0)),
                    pl.BlockSpec((D, tn), lambda j, i: (0, j))],
        out_specs=pl.BlockSpec((tm, tn), lambda j, i: (i, j)),
        out_shape=jax.ShapeDtypeStruct((ntot, ncol), F32),
        compiler_params=_params("parallel", "parallel"),
        name="in_proj",
    )(*hs, mod, w)


def _mla_proj_kernel(cq_ref, ckv_ref, kr_ref, qn_ref, kn_ref, wq_ref, wkv_ref, cs1_ref, cs2_ref,
                     q_ref, k_ref, v_ref):
    cs1 = cs1_ref[...]
    cs2 = cs2_ref[...]

    def rms(x, g):
        return (x * lax.rsqrt(jnp.mean(x * x, axis=-1, keepdims=True) + 1e-6) * g).astype(BF16)

    def rope(g):
        return g * cs1 + pltpu.roll(g, MLA_ROPE, 1) * cs2

    oq = _dot(rms(cq_ref[...], qn_ref[...]), wq_ref[...])
    okv = _dot(rms(ckv_ref[...], kn_ref[...]), wkv_ref[...])
    krot = rope(kr_ref[...])
    for hd in range(MLA_HEADS):
        c0 = hd * MLA_QK_PAD
        qn = oq[:, c0:c0 + MLA_NOPE]
        qr = rope(oq[:, c0 + MLA_NOPE:c0 + MLA_QK_PAD])
        q_ref[hd] = (jnp.concatenate([qn, qr], axis=1) * Q_SCALE).astype(BF16)
        k_ref[hd] = jnp.concatenate([okv[:, c0:c0 + MLA_NOPE], krot], axis=1).astype(BF16)
        v_ref[hd] = okv[:, c0 + MLA_NOPE:c0 + MLA_QK_PAD].T.astype(BF16)


def _mla_proj(p, q_norm, kv_norm, wq, wkv, cs1, cs2, n_lat, t_len, tm):
    ntot = p.shape[0]
    H = MLA_HEADS
    npos = t_len // tm

    def pos(i):
        return jnp.where(i * tm < n_lat, i % npos, npos)

    return pl.pallas_call(
        _mla_proj_kernel,
        grid=(ntot // tm,),
        in_specs=[pl.BlockSpec((tm, MLA_RANK), lambda i: (i, COL_CQ // MLA_RANK)),
                  pl.BlockSpec((tm, MLA_RANK), lambda i: (i, COL_CKV // MLA_RANK)),
                  pl.BlockSpec((tm, LANES), lambda i: (i, COL_KR // LANES)),
                  pl.BlockSpec((1, MLA_RANK), lambda i: (0, 0)),
                  pl.BlockSpec((1, MLA_RANK), lambda i: (0, 0)),
                  pl.BlockSpec(wq.shape, lambda i: (0, 0)),
                  pl.BlockSpec(wkv.shape, lambda i: (0, 0)),
                  pl.BlockSpec((tm, LANES), lambda i: (pos(i), 0)),
                  pl.BlockSpec((tm, LANES), lambda i: (pos(i), 0))],
        out_specs=[pl.BlockSpec((H, tm, MLA_QK_PAD), lambda i: (0, i, 0)),
                   pl.BlockSpec((H, tm, MLA_QK_PAD), lambda i: (0, i, 0)),
                   pl.BlockSpec((H, MLA_V, tm), lambda i: (0, 0, i))],
        out_shape=[jax.ShapeDtypeStruct((H, ntot, MLA_QK_PAD), BF16),
                   jax.ShapeDtypeStruct((H, ntot, MLA_QK_PAD), BF16),
                   jax.ShapeDtypeStruct((H, MLA_V, ntot), BF16)],
        compiler_params=_params("parallel"),
        name="mla_proj",
    )(p, p, p, q_norm, kv_norm, wq, wkv, cs1, cs2)


def _softmax_pv(scores, values):
    def rows(x):
        return x.reshape(x.shape[0] // SUBLANES, SUBLANES, x.shape[1])

    m = functools.reduce(jnp.maximum, [jnp.max(rows(s), axis=0) for s in scores])
    m = jnp.max(m, axis=0, keepdims=True)
    ps = [jnp.exp2(s - m) for s in scores]
    l = functools.reduce(jnp.add, [jnp.sum(rows(p), axis=0) for p in ps])
    l = jnp.sum(l, axis=0, keepdims=True)
    acc = functools.reduce(jnp.add, [_dot(v, p.astype(BF16)) for p, v in zip(ps, values)])
    return (acc / l).T


def _attn_kernel(q_ref, kl_ref, kc_ref, vl_ref, vc_ref, o_ref, *, nq_lat, n_chunks):
    qi = pl.program_id(2)
    heads = range(ATTN_HEADS_PER_STEP)

    @pl.when(qi < nq_lat)
    def _():
        scores, values = [], []
        for hd in heads:
            q = q_ref[hd]
            sc, va = [_dot_nt(kc_ref[hd], q)], [vc_ref[hd]]
            for c in range(n_chunks):
                sc.append(_dot_nt(kl_ref[hd, c * ATTN_TK:(c + 1) * ATTN_TK, :], q))
                va.append(vl_ref[hd, :, c * ATTN_TK:(c + 1) * ATTN_TK])
            scores.append(sc)
            values.append(va)
        for hd in heads:
            o_ref[:, hd * MLA_V:(hd + 1) * MLA_V] = _softmax_pv(scores[hd], values[hd]).astype(o_ref.dtype)

    @pl.when(qi >= nq_lat)
    def _():
        for hd in heads:
            o = _softmax_pv([_dot_nt(kc_ref[hd], q_ref[hd])], [vc_ref[hd]])
            o_ref[:, hd * MLA_V:(hd + 1) * MLA_V] = o.astype(o_ref.dtype)


def _attention(qh, kh, vh, n_batch, t_len, ctx_len, with_ctx):
    H, ntot, _ = qh.shape
    n_lat = n_batch * t_len
    tq = ATTN_TQ
    hp = ATTN_HEADS_PER_STEP
    assert ctx_len == tq and t_len % ATTN_TK == 0 and H % hp == 0
    nq_lat = t_len // tq
    nq = nq_lat + (1 if with_ctx else 0)
    ctx_blk0 = n_lat // ctx_len

    def qrow(b, qi):
        return jnp.where(qi < nq_lat, b * nq_lat + qi, n_lat // tq + b)

    kern = functools.partial(_attn_kernel, nq_lat=nq_lat, n_chunks=t_len // ATTN_TK)
    return pl.pallas_call(
        kern,
        grid=(n_batch, H // hp, nq),
        in_specs=[pl.BlockSpec((hp, tq, MLA_QK_PAD), lambda b, h, qi: (h, qrow(b, qi), 0)),
                  pl.BlockSpec((hp, t_len, MLA_QK_PAD), lambda b, h, qi: (h, b, 0)),
                  pl.BlockSpec((hp, ctx_len, MLA_QK_PAD), lambda b, h, qi: (h, ctx_blk0 + b, 0)),
                  pl.BlockSpec((hp, MLA_V, t_len), lambda b, h, qi: (h, 0, b)),
                  pl.BlockSpec((hp, MLA_V, ctx_len), lambda b, h, qi: (h, 0, ctx_blk0 + b))],
        out_specs=pl.BlockSpec((tq, hp * MLA_V), lambda b, h, qi: (qrow(b, qi), h)),
        out_shape=jax.ShapeDtypeStruct((ntot, H * MLA_V), BF16),
        compiler_params=_params("parallel", "parallel", "arbitrary"),
        name="mla_attention",
    )(qh, kh, kh, vh, vh)


def _tri(n, reverse, transpose=False):
    i = lax.broadcasted_iota(jnp.int32, (n, n), 0)
    j = lax.broadcasted_iota(jnp.int32, (n, n), 1)
    if transpose:
        i, j = j, i
    return ((j >= i) if reverse else (j <= i)).astype(F32)


def _interleave(gens, skew):
    out = [None] * len(gens)
    live = set(range(len(gens)))
    tick = 0
    while live:
        for c in sorted(live):
            if tick < c * skew:
                continue
            try:
                next(gens[c])
            except StopIteration as stop:
                out[c] = stop.value
                live.discard(c)
        tick += 1
    return out


def _tri_inverse_consts(n, size):
    sub = SUBLANES
    ii = lax.broadcasted_iota(jnp.int32, (n, n), 0)
    jj = lax.broadcasted_iota(jnp.int32, (n, n), 1)
    diag = (ii // sub) == (jj // sub)
    lane = lax.broadcasted_iota(jnp.int32, (sub, n), 1) % sub
    row = lax.broadcasted_iota(jnp.int32, (sub, n), 0)
    offs = []
    s = sub
    while s < size:
        offs.append(((ii // (2 * s)) == (jj // (2 * s))) & ((ii // s) != (jj // s)))
        s *= 2
    return diag, lane, row, offs


def _diag_block_inverse(nmat, rev, consts):
    diag, lane, row, _ = consts
    n = nmat.shape[0]
    sub = SUBLANES
    nc = jnp.sum(jnp.where(diag, nmat, 0.0).reshape(n // sub, sub, n), axis=0)
    cols = []
    for j in range(sub):
        one = jnp.where(lane == j, nc, 0.0)
        up, dn = one, one
        for sh in (1, 2, 4):
            up = up + pltpu.roll(up, sh, 1)
            dn = dn + pltpu.roll(dn, n - sh, 1)
        cols.append(jnp.where(lane >= j, up, dn))
    x = (lane == row).astype(F32)
    for j in (range(sub - 1, 0, -1) if rev else range(sub - 1)):
        x = x + cols[j] * x[j:j + 1, :]
    return jnp.where(diag, jnp.broadcast_to(x[None], (n // sub, sub, n)).reshape(n, n), 0.0)


def _unit_tri_inverse_steps(nmat, rev, consts):
    x = _diag_block_inverse(nmat, rev, consts)
    yield
    for off in consts[3]:
        xb = x.astype(BF16)
        t1 = _dot(xb, jnp.where(off, nmat, 0.0).astype(BF16)).astype(BF16)
        yield
        x = x + _dot(t1, xb)
        yield
    return x


def _unit_tri_inverses(nmats, revs, size):
    consts = _tri_inverse_consts(nmats[0].shape[0], size)
    return _interleave([_unit_tri_inverse_steps(nm, rev, consts) for nm, rev in zip(nmats, revs)], CHAIN_SKEW)


def _rwkv_chunks(chains, revs):
    C = RWKV_CHUNK
    n2 = 2 * C
    tri = {rev: _tri(C, rev).astype(BF16) for rev in set(revs)}
    lane = lax.broadcasted_iota(jnp.int32, (C, LANES), 1)
    m0 = (lane < RWKV_HEAD).astype(F32)
    m1 = 1.0 - m0
    ii = lax.broadcasted_iota(jnp.int32, (n2, n2), 0)
    jj = lax.broadcasted_iota(jnp.int32, (n2, n2), 1)
    same = (ii // C) == (jj // C)
    il, jl = ii % C, jj % C
    strict = {rev: same & ((jl > il) if rev else (jl < il)) for rev in set(revs)}
    incl = {rev: same & ((jl >= il) if rev else (jl <= il)) for rev in set(revs)}
    inv_consts = _tri_inverse_consts(n2, C)

    def ex(x):
        return jnp.concatenate([x * m0, x * m1], axis=0)

    def chain(r, v, a, lw, kd, bd, rev):
        cum = _dot_split3(tri[rev], lw)
        tot = jnp.sum(lw, axis=0, keepdims=True)
        yield
        e_neg = jnp.exp(-cum)
        e_rem = jnp.exp(tot - cum)
        at, rt = ex(a * jnp.exp(cum - lw)), ex(r * jnp.exp(cum))
        bt, kt = ex(bd * e_neg), ex(kd * e_neg)
        bh, kh = ex(bd * e_rem).astype(BF16), ex(kd * e_rem).astype(BF16)
        ve = ex(v).astype(BF16)
        yield
        aa = _dot_nt(jnp.concatenate([at, rt], axis=0).astype(BF16), jnp.concatenate([bt, kt], axis=0).astype(BF16))
        yield
        a_ab = jnp.where(strict[rev], aa[:n2, :n2], 0.0)
        a_ak = jnp.where(strict[rev], aa[:n2, n2:], 0.0).astype(BF16)
        a_rb = jnp.where(incl[rev], aa[n2:, :n2], 0.0).astype(BF16)
        a_rk = jnp.where(incl[rev], aa[n2:, n2:], 0.0).astype(BF16)
        akv = _dot(a_ak, ve)
        y0b = _dot(a_rk, ve)
        yield
        tinv = yield from _unit_tri_inverse_steps(a_ab, rev, inv_consts)
        zb = _dot(tinv.astype(BF16), jnp.concatenate([at, akv], axis=1).astype(BF16)).astype(BF16)
        yield
        w2 = _dot(a_rb, zb)
        gmt = _dot_tn(zb[:, :LANES], bh).astype(BF16)
        s0t = _dot_tn(jnp.concatenate([zb[:, LANES:], ve], axis=0), jnp.concatenate([bh, kh], axis=0))
        yield
        rhat = (rt + w2[:, :LANES]).astype(BF16)
        y0 = w2[:, LANES:] + y0b
        return rhat, y0, gmt, s0t, jnp.exp(tot)

    return _interleave([chain(*c, rev) for c, rev in zip(chains, revs)], CHAIN_SKEW)


def _rwkv_apply(pre, states):
    C = RWKV_CHUNK
    sbs = [s.astype(BF16) for s in states]
    yes = [_dot_nt(rh, sb) + y0 for (rh, y0, _, _, _), sb in zip(pre, sbs)]
    ys = [ye[:C] + ye[C:] for ye in yes]
    new_states = [s * dec + _dot(sb, g) + s0 for (_, _, g, s0, dec), s, sb in zip(pre, states, sbs)]
    return ys, new_states


def _rwkv_scan_kernel(rf, rb, vf, vb, af, ab, lw0, lw1, kd0, kd1, b0, b1, yf_ref, yb_ref, s_ref):
    @pl.when(pl.program_id(1) == 0)
    def _():
        s_ref[...] = jnp.zeros_like(s_ref)

    C = RWKV_CHUNK
    nsub = rf.shape[0] // C
    dirs = ((rf, vf, af, lw0, kd0, b0, yf_ref), (rb, vb, ab, lw1, kd1, b1, yb_ref))
    npair = RWKV_WIDTH // LANES
    chains, revs, where = [], [], []
    for d, (r_, v_, a_, lw_, kd_, b_, y_) in enumerate(dirs):
        for j in range(npair):
            sl = slice(j * LANES, (j + 1) * LANES)
            for c in range(nsub):
                rows = slice(c * C, (c + 1) * C)
                chains.append(tuple(ref[rows, sl] for ref in (r_, v_, a_, lw_, kd_, b_)))
                revs.append(d == 1)
                where.append((d, j, c))
    pre = dict(zip(where, _rwkv_chunks(chains, revs)))
    keys = [(d, j) for d in range(2) for j in range(npair)]
    states = [s_ref[d, j] for d, j in keys]
    for step in range(nsub):
        sub = [step if d == 0 else nsub - 1 - step for d, _ in keys]
        ys, states = _rwkv_apply([pre[(d, j, c)] for (d, j), c in zip(keys, sub)], states)
        for (d, j), c, y in zip(keys, sub, ys):
            dirs[d][6][c * C:(c + 1) * C, j * LANES:(j + 1) * LANES] = y
    for (d, j), s_new in zip(keys, states):
        s_ref[d, j] = s_new


def _chunk_maps(n_batch, t_len, ctx_len, chunk):
    nctx, nlat = ctx_len // chunk, t_len // chunk
    base = n_batch * nlat

    def fwd(b, i):
        return jnp.where(i < nctx, base + b * nctx + i, b * nlat + i - nctx)

    def bwd(b, i):
        return jnp.where(i < nctx, base + b * nctx + (nctx - 1 - i), b * nlat + (nlat - 1 - (i - nctx)))

    return fwd, bwd, nctx + nlat


def _rwkv_scan(r, v, a, lw0, lw1, kd0, kd1, b0, b1, n_batch, t_len, ctx_len):
    ntot, W = r.shape
    C = RWKV_BLOCK
    fwd, bwd, nsteps = _chunk_maps(n_batch, t_len, ctx_len, C)
    fs = pl.BlockSpec((C, W), lambda b, i: (fwd(b, i), 0))
    bs = pl.BlockSpec((C, W), lambda b, i: (bwd(b, i), 0))
    return pl.pallas_call(
        _rwkv_scan_kernel,
        grid=(n_batch, nsteps),
        in_specs=[fs, bs, fs, bs, fs, bs, fs, bs, fs, bs, fs, bs],
        out_specs=[fs, bs],
        out_shape=[jax.ShapeDtypeStruct((ntot, W), F32)] * 2,
        scratch_shapes=[pltpu.VMEM((2, W // LANES, LANES, LANES), F32)],
        compiler_params=_params("parallel", "arbitrary"),
        name="rwkv7_scan",
    )(r, r, v, v, a, a, lw0, lw1, kd0, kd1, b0, b1)


def _gdn_scan_kernel(qf, qb, kf, kb, vf, vb, gf, gb, of_ref, ob_ref, s_ref):
    @pl.when(pl.program_id(1) == 0)
    def _():
        s_ref[...] = jnp.zeros_like(s_ref)

    C = GDN_CHUNK
    DH = GDN_HEAD
    ii = lax.broadcasted_iota(jnp.int32, (C, C), 0)
    jj = lax.broadcasted_iota(jnp.int32, (C, C), 1)
    strict = {False: jj < ii, True: jj > ii}
    incl = {False: jj <= ii, True: jj >= ii}
    nsub = qf.shape[0] // C
    dirs = ((qf, kf, vf, gf, of_ref), (qb, kb, vb, gb, ob_ref))
    qs, ks, vs, gcs, grs, gts, betas, revs, where = [], [], [], [], [], [], [], [], []
    for d, (q_, k_, v_, g_, o_) in enumerate(dirs):
        rev = d == 1
        for c in range(nsub):
            rows = slice(c * C, (c + 1) * C)
            gbv = g_[rows, :]
            gcol = jnp.dot(_tri(C, rev), gbv, precision=HIGHEST, preferred_element_type=F32)
            grow = jnp.dot(gbv.T, _tri(C, rev, transpose=True), precision=HIGHEST, preferred_element_type=F32)
            gtot = jnp.sum(gbv, axis=0, keepdims=True)
            for hd in range(GDN_HEADS):
                sl = slice(hd * DH, (hd + 1) * DH)
                cg = d * GDN_HEADS + hd
                cb = 2 * GDN_HEADS + cg
                qs.append(q_[rows, sl]); ks.append(k_[rows, sl]); vs.append(v_[rows, sl])
                gcs.append(gcol[:, cg:cg + 1]); grs.append(grow[cg:cg + 1, :]); gts.append(gtot[:, cg:cg + 1])
                betas.append(gbv[:, cb:cb + 1]); revs.append(rev); where.append((d, hd, c))

    decs = [jnp.exp(jnp.where(incl[rev], gc - gr, -1e30)) for gc, gr, rev in zip(gcs, grs, revs)]
    kbetas = [k * b for k, b in zip(ks, betas)]
    qks = [_dot_nt(jnp.concatenate([kb_, q], axis=0).astype(BF16), k.astype(BF16)) for kb_, q, k in zip(kbetas, qs, ks)]
    nmats = [jnp.where(strict[rev], -qk[:C] * dec, 0.0) for qk, dec, rev in zip(qks, decs, revs)]
    aints = [jnp.where(incl[rev], qk[C:] * dec, 0.0).astype(BF16) for qk, dec, rev in zip(qks, decs, revs)]
    egs = [jnp.exp(gc) for gc in gcs]
    rhss = [jnp.concatenate([v * b, kb_ * eg], axis=1).astype(BF16) for v, b, kb_, eg in zip(vs, betas, kbetas, egs)]
    khats = [(k * jnp.exp(gt - gc)).astype(BF16) for k, gt, gc in zip(ks, gts, gcs)]
    tinvs = _unit_tri_inverses(nmats, revs, C)
    sols = [_dot(ti.astype(BF16), rhs).astype(BF16) for ti, rhs in zip(tinvs, rhss)]
    qos = [_dot(ai, sol) for ai, sol in zip(aints, sols)]
    qhats = [(q * eg - qo[:, DH:]).astype(BF16) for q, eg, qo in zip(qs, egs, qos)]
    gss = [_dot_tn(kh, sol) for kh, sol in zip(khats, sols)]
    pre = dict(zip(where, zip(qhats, qos, gss, [jnp.exp(gt) for gt in gts])))

    keys = [(d, hd) for d in range(2) for hd in range(GDN_HEADS)]
    states = [s_ref[d, hd] for d, hd in keys]
    for step in range(nsub):
        sub = [step if d == 0 else nsub - 1 - step for d, _ in keys]
        cur = [pre[(d, hd, c)] for (d, hd), c in zip(keys, sub)]
        sbs = [s.astype(BF16) for s in states]
        os_ = [_dot(qh, sb) + qo[:, :DH] for (qh, qo, _, _), sb in zip(cur, sbs)]
        states = [s * dec - _dot(gs[:, DH:].astype(BF16), sb) + gs[:, :DH]
                  for (_, _, gs, dec), s, sb in zip(cur, states, sbs)]
        for (d, hd), c, o in zip(keys, sub, os_):
            dirs[d][4][c * C:(c + 1) * C, hd * DH:(hd + 1) * DH] = o
    for (d, hd), s_new in zip(keys, states):
        s_ref[d, hd] = s_new


def _gdn_scan(q, k, v, gb, n_batch, t_len, ctx_len):
    ntot, W = q.shape
    C = GDN_BLOCK
    fwd, bwd, nsteps = _chunk_maps(n_batch, t_len, ctx_len, C)
    fs = pl.BlockSpec((C, W), lambda b, i: (fwd(b, i), 0))
    bs = pl.BlockSpec((C, W), lambda b, i: (bwd(b, i), 0))
    gfs = pl.BlockSpec((C, LANES), lambda b, i: (fwd(b, i), 0))
    gbs = pl.BlockSpec((C, LANES), lambda b, i: (bwd(b, i), 0))
    return pl.pallas_call(
        _gdn_scan_kernel,
        grid=(n_batch, nsteps),
        in_specs=[fs, bs, fs, bs, fs, bs, gfs, gbs],
        out_specs=[fs, bs],
        out_shape=[jax.ShapeDtypeStruct((ntot, W), F32)] * 2,
        scratch_shapes=[pltpu.VMEM((2, GDN_HEADS, GDN_HEAD, GDN_HEAD), F32)],
        compiler_params=_params("parallel", "arbitrary"),
        name="gdn_scan",
    )(q, q, k, k, v, v, gb, gb)


def _outproj_kernel(*refs, alpha, n_src, n_lat_tiles):
    (a_ref, yf_ref, yb_ref, r_ref, v_ref, kd0_ref, kd1_ref, g_ref, of_ref, ob_ref, z_ref,
     rk_ref, gng_ref, gnb_ref, gdn_ref, seg_ref, w_ref, mod_ref, lng_ref, lnb_ref, o_ref) = refs[n_src:]
    seg = seg_ref[...]
    y = yf_ref[...] + yb_ref[...]
    yc = y - _seg_sum(y, seg) * (1.0 / RWKV_HEAD)
    var = _seg_sum(yc * yc, seg) * (1.0 / RWKV_HEAD)
    yn = yc * lax.rsqrt(var + RWKV_GN_EPS) * gng_ref[...] + gnb_ref[...]
    bonus = _seg_sum(r_ref[...] * (kd0_ref[...] + kd1_ref[...]) * rk_ref[...], seg) * v_ref[...]
    b_mix = ((yn + bonus) * g_ref[...]).astype(BF16)
    o = of_ref[...] + ob_ref[...]
    z = z_ref[...]
    parts = []
    for hd in range(GDN_HEADS):
        oh = o[:, hd * GDN_HEAD:(hd + 1) * GDN_HEAD]
        parts.append(oh * lax.rsqrt(jnp.mean(oh * oh, axis=-1, keepdims=True) + 1e-6))
    c_mix = (jnp.concatenate(parts, axis=1) * gdn_ref[...] * (z * jax.nn.sigmoid(z))).astype(BF16)

    na, nb = a_ref.shape[1], RWKV_WIDTH
    acc = _dot(a_ref[...], w_ref[0:na, :])
    acc += _dot(b_mix, w_ref[na:na + nb, :])
    acc += _dot(c_mix, w_ref[na + nb:, :])
    gate = mod_ref[0][2:3]
    x = _load_stream(refs[:n_src], pl.program_id(0), n_lat_tiles)
    o_ref[...] = _layer_norm(alpha * x + gate * acc, lng_ref[...], lnb_ref[...])


def _out_proj(hs, a, rw, gd, p, r_k, gn_g, gn_b, gdn_norm, seg, w, mod, ln_g, ln_b,
              n_rows, n_lat, t_len, n_batch, tm, alpha):
    D = hs[0].shape[1]
    W = RWKV_WIDTH

    def mod_row(i):
        return jnp.where(i * tm < n_lat, (i * tm) // t_len, n_batch)

    tok = pl.BlockSpec((tm, W), lambda i: (i, 0))
    vec = pl.BlockSpec((1, W), lambda i: (0, 0))
    return pl.pallas_call(
        functools.partial(_outproj_kernel, alpha=alpha, n_src=len(hs), n_lat_tiles=n_lat // tm),
        grid=(n_rows // tm,),
        in_specs=_stream_specs(hs, tm, n_lat, lambda i: i)
                 + [pl.BlockSpec((tm, a.shape[1]), lambda i: (i, 0))]
                 + [tok] * 9
                 + [pl.BlockSpec((tm, W), lambda i: (i, COL_Z // W)),
                    vec, vec, vec, vec,
                    pl.BlockSpec(seg.shape, lambda i: (0, 0)),
                    pl.BlockSpec(w.shape, lambda i: (0, 0)),
                    pl.BlockSpec((1, 8, D), lambda i: (mod_row(i), 0, 0)),
                    pl.BlockSpec((1, D), lambda i: (0, 0)),
                    pl.BlockSpec((1, D), lambda i: (0, 0))],
        out_specs=pl.BlockSpec((tm, D), lambda i: (i, 0)),
        out_shape=jax.ShapeDtypeStruct((n_rows, D), F32),
        compiler_params=_params("parallel"),
        name="out_proj_ln",
    )(*hs, a, *rw, *gd, p, r_k, gn_g, gn_b, gdn_norm, seg, w, mod, ln_g, ln_b)


def _ffn_kernel(x_ref, mod_ref, wg_ref, wu_ref, wd_ref, g_ref, beta_ref, o_ref, xm_ref, acc_ref, *, alpha):
    f = pl.program_id(1)

    @pl.when(f == 0)
    def _():
        m = mod_ref[0]
        xm_ref[...] = (x_ref[...] * (1.0 + m[4:5]) + m[3:4]).astype(BF16)
        acc_ref[...] = jnp.zeros_like(acc_ref)

    xm = xm_ref[...]
    gate = _dot(xm, wg_ref[...])
    up = _dot(xm, wu_ref[...])
    act = gate * jax.nn.sigmoid(gate) * up
    acc_ref[...] += _dot(act.astype(BF16), wd_ref[...])

    @pl.when(f == pl.num_programs(1) - 1)
    def _():
        m = mod_ref[0]
        o_ref[...] = _layer_norm(alpha * x_ref[...] + m[5:6] * acc_ref[...], g_ref[...], beta_ref[...])


def _ffn(h, mod, wg, wu, wd, ln_g, ln_b, n_rows, n_lat, t_len, n_batch, tm, tf, alpha):
    D = h.shape[1]
    F = wg.shape[1]

    def mod_row(i):
        return jnp.where(i * tm < n_lat, (i * tm) // t_len, n_batch)

    return pl.pallas_call(
        functools.partial(_ffn_kernel, alpha=alpha),
        grid=(n_rows // tm, F // tf),
        in_specs=[pl.BlockSpec((tm, D), lambda i, f: (i, 0)),
                  pl.BlockSpec((1, 8, D), lambda i, f: (mod_row(i), 0, 0)),
                  pl.BlockSpec((D, tf), lambda i, f: (0, f)),
                  pl.BlockSpec((D, tf), lambda i, f: (0, f)),
                  pl.BlockSpec((tf, D), lambda i, f: (f, 0)),
                  pl.BlockSpec((1, D), lambda i, f: (0, 0)),
                  pl.BlockSpec((1, D), lambda i, f: (0, 0))],
        out_specs=pl.BlockSpec((tm, D), lambda i, f: (i, 0)),
        out_shape=jax.ShapeDtypeStruct((n_rows, D), F32),
        scratch_shapes=[pltpu.VMEM((tm, D), BF16), pltpu.VMEM((tm, D), F32)],
        compiler_params=_params("parallel", "arbitrary"),
        name="ffn_ln",
    )(h, mod, wg, wu, wd, ln_g, ln_b)


def _in_proj_columns():
    kr0 = 2 * MLA_RANK
    swap = np.concatenate([np.arange(16, 32), np.arange(0, 16), np.arange(48, 64), np.arange(32, 48)])
    pr0 = kr0 + MLA_ROPE
    n_lora = 2 * RWKV_LORA_W + 2 * RWKV_LORA_A + RWKV_LORA_G
    pg0 = pr0 + 3 * RWKV_WIDTH + n_lora
    nh4 = 4 * GDN_HEADS
    idx = -np.ones(IN_COLS_PAD, np.int64)

    def put(dst, src):
        idx[dst:dst + len(src)] = src

    put(COL_RKV, pr0 + np.arange(3 * RWKV_WIDTH))
    put(COL_GQKV, pg0 + np.arange(3 * GDN_WIDTH))
    put(COL_Z, pg0 + 3 * GDN_WIDTH + np.arange(GDN_WIDTH))
    put(COL_CQ, np.arange(MLA_RANK))
    put(COL_CKV, MLA_RANK + np.arange(MLA_RANK))
    put(COL_LORA, pr0 + 3 * RWKV_WIDTH + np.arange(n_lora))
    put(COL_KR, kr0 + np.arange(MLA_ROPE))
    put(COL_KR + MLA_ROPE, kr0 + swap)
    put(COL_GAB, pg0 + 4 * GDN_WIDTH + np.arange(nh4))
    return idx


def _take_cols(w, idx):
    safe = jnp.asarray(np.maximum(idx, 0), jnp.int32)
    mask = jnp.asarray(idx >= 0)
    return jnp.where(mask, jnp.take(w, safe, axis=-1), 0.0)


def _uq_columns():
    swap = np.concatenate([np.arange(16, 32), np.arange(0, 16), np.arange(48, 64), np.arange(32, 48)])
    per = MLA_NOPE + MLA_ROPE
    out = []
    for hd in range(MLA_HEADS):
        out += [hd * per + np.arange(MLA_NOPE), hd * per + MLA_NOPE + np.arange(MLA_ROPE), hd * per + MLA_NOPE + swap]
    return np.concatenate(out)


def _rope_tables(t_len, tm):
    rows = t_len // GRID_W
    row = jnp.repeat(jnp.arange(rows), GRID_W)
    col = jnp.tile(jnp.arange(GRID_W), rows)
    n_freq = MLA_ROPE // 4
    inv = ROPE_THETA ** (-jnp.arange(n_freq, dtype=F32) / n_freq)
    ar = row.astype(F32)[:, None] * inv
    ac = col.astype(F32)[:, None] * inv
    cos = jnp.concatenate([jnp.cos(ar), jnp.cos(ar), jnp.cos(ac), jnp.cos(ac)], axis=1)
    sin = jnp.concatenate([-jnp.sin(ar), jnp.sin(ar), -jnp.sin(ac), jnp.sin(ac)], axis=1)
    zer = jnp.zeros((t_len, LANES - MLA_ROPE), F32)
    cs1 = jnp.concatenate([cos, zer], axis=1)
    cs2 = jnp.concatenate([sin, zer], axis=1)
    ident = jnp.concatenate([jnp.ones((tm, MLA_ROPE), F32), jnp.zeros((tm, LANES - MLA_ROPE), F32)], axis=1)
    return jnp.concatenate([cs1, ident], axis=0), jnp.concatenate([cs2, jnp.zeros((tm, LANES), F32)], axis=0)


def _fill_halo(ext_ref, cur_ref, prev_ref, next_ref, first, last):
    tm = cur_ref.shape[0]
    ext_ref[0:HALO, :] = jnp.where(first, 0.0, prev_ref[...])
    ext_ref[HALO:HALO + tm, :] = cur_ref[...]
    ext_ref[HALO + tm:, :] = jnp.where(last, 0.0, next_ref[...])


def _seq_edges(tm, n_lat, t_len, ctx_len):
    t0 = pl.program_id(0) * tm
    lat = t0 < n_lat
    first = jnp.where(lat, t0 % t_len == 0, (t0 - n_lat) % ctx_len == 0)
    last = jnp.where(lat, (t0 + tm) % t_len == 0, (t0 - n_lat + tm) % ctx_len == 0)
    return first, last


def _softplus(x):
    return jnp.maximum(x, 0.0) + jnp.log(1.0 + jnp.exp(-jnp.abs(x)))


def _rwkv_prep_kernel(rkv_ref, rkv_p, rkv_n, lg_ref, lg_p, lg_n, mu_rkv, mu_lg, w0_ref, a0_ref, w2_ref, a2_ref,
                      g2_ref, kk_ref, ka_ref, seg_ref,
                      r_o, v_o, a_o, lw0_o, lw1_o, kd0_o, kd1_o, b0_o, b1_o, g_o, ext_rkv, ext_lg,
                      *, n_lat, t_len, ctx_len):
    tm = rkv_ref.shape[0]
    W = RWKV_WIDTH
    first, last = _seq_edges(tm, n_lat, t_len, ctx_len)
    _fill_halo(ext_rkv, rkv_ref, rkv_p, rkv_n, first, last)
    _fill_halo(ext_lg, lg_ref, lg_p, lg_n, first, last)

    def mixed(ext, mu):
        cur = ext[HALO:HALO + tm, :]
        sh = 0.5 * (ext[HALO - 1:HALO - 1 + tm, :] + ext[HALO + 1:HALO + 1 + tm, :])
        return cur + (sh - cur) * mu[...]

    p = mixed(ext_rkv, mu_rkv)
    lg = mixed(ext_lg, mu_lg)
    r, k, v = p[:, :W], p[:, W:2 * W], p[:, 2 * W:]
    lora = lg[:, :LANES]
    tw = jnp.tanh(lora).astype(BF16)
    ab = lora.astype(BF16)
    kkr = k * kk_ref[...]
    kk = kkr * lax.rsqrt(_seg_sum(kkr * kkr, seg_ref[...]) + 1e-12)
    r_o[...] = r
    v_o[...] = v
    a_o[...] = -kk
    g_o[...] = _dot(jax.nn.sigmoid(lg[:, LANES:]).astype(BF16), g2_ref[...])
    for d, (lw_o, kd_o, b_o) in enumerate(((lw0_o, kd0_o, b0_o), (lw1_o, kd1_o, b1_o))):
        w_log = -_softplus(-(w0_ref[d:d + 1, :] + _dot(tw, w2_ref[d]))) - 0.5
        lw_o[...] = -jnp.exp(w_log)
        a_lr = jax.nn.sigmoid(a0_ref[d:d + 1, :] + _dot(ab, a2_ref[d]))
        kd_o[...] = k * (1.0 + (a_lr - 1.0) * ka_ref[...])
        b_o[...] = kk * a_lr


def _halo_specs(tm, width, col, ntot):
    nb = tm // HALO
    last_blk = ntot // HALO - 1
    return [pl.BlockSpec((tm, width), lambda i: (i, col)),
            pl.BlockSpec((HALO, width), lambda i: (jnp.maximum(i * nb - 1, 0), col)),
            pl.BlockSpec((HALO, width), lambda i: (jnp.minimum((i + 1) * nb, last_blk), col))]


def _rwkv_prep(p, mu_rkv, mu_lg, w0, a0, w2p, a2p, g2p, k_k, k_a, seg, n_lat, t_len, ctx_len, tm):
    ntot = p.shape[0]
    W = RWKV_WIDTH
    full = lambda arr: pl.BlockSpec(arr.shape, lambda i: (0,) * arr.ndim)
    params = (mu_rkv, mu_lg, w0, a0, w2p, a2p, g2p, k_k, k_a, seg)
    return pl.pallas_call(
        functools.partial(_rwkv_prep_kernel, n_lat=n_lat, t_len=t_len, ctx_len=ctx_len),
        grid=(ntot // tm,),
        in_specs=_halo_specs(tm, 3 * W, COL_RKV // (3 * W), ntot) + _halo_specs(tm, 2 * LANES, COL_LORA // (2 * LANES), ntot)
                 + [full(a) for a in params],
        out_specs=[pl.BlockSpec((tm, W), lambda i: (i, 0))] * 10,
        out_shape=[jax.ShapeDtypeStruct((ntot, W), F32)] * 10,
        scratch_shapes=[pltpu.VMEM((tm + 2 * HALO, 3 * W), F32), pltpu.VMEM((tm + 2 * HALO, 2 * LANES), F32)],
        compiler_params=_params("parallel"),
        name="rwkv7_prep",
    )(p, p, p, p, p, p, *params)


def _gdn_prep_kernel(x_ref, x_p, x_n, ab_ref, cw_ref, alog_ref, dt_ref, q_o, k_o, v_o, gb_o, ext,
                     *, n_lat, t_len, ctx_len):
    tm = x_ref.shape[0]
    W = GDN_WIDTH
    first, last = _seq_edges(tm, n_lat, t_len, ctx_len)
    _fill_halo(ext, x_ref, x_p, x_n, first, last)
    half = GDN_CONV // 2
    conv = functools.reduce(jnp.add, [ext[HALO - half + j:HALO - half + j + tm, :] * cw_ref[j:j + 1, :]
                                      for j in range(GDN_CONV)])
    act = conv * jax.nn.sigmoid(conv)

    def l2n(x):
        parts = []
        for hd in range(GDN_HEADS):
            xh = x[:, hd * GDN_HEAD:(hd + 1) * GDN_HEAD]
            parts.append(xh * lax.rsqrt(jnp.sum(xh * xh, axis=-1, keepdims=True) + 1e-12))
        return jnp.concatenate(parts, axis=1)

    q_o[...] = l2n(act[:, :W]) * GDN_HEAD ** -0.5
    k_o[...] = l2n(act[:, W:2 * W])
    v_o[...] = act[:, 2 * W:]
    ab = ab_ref[...]
    lane = lax.broadcasted_iota(jnp.int32, ab.shape, 1)
    nh2 = 2 * GDN_HEADS
    glog = -jnp.exp(alog_ref[...]) * _softplus(ab + dt_ref[...])
    gb_o[...] = jnp.where(lane < nh2, glog, jnp.where(lane < 2 * nh2, jax.nn.sigmoid(ab), 0.0))


def _gdn_prep(p, conv_w, a_log, dt_bias, n_lat, t_len, ctx_len, tm):
    ntot = p.shape[0]
    W = GDN_WIDTH
    full = lambda arr: pl.BlockSpec(arr.shape, lambda i: (0,) * arr.ndim)
    return pl.pallas_call(
        functools.partial(_gdn_prep_kernel, n_lat=n_lat, t_len=t_len, ctx_len=ctx_len),
        grid=(ntot // tm,),
        in_specs=_halo_specs(tm, 3 * W, COL_GQKV // (3 * W), ntot)
                 + [pl.BlockSpec((tm, LANES), lambda i: (i, COL_GAB // LANES)), full(conv_w), full(a_log), full(dt_bias)],
        out_specs=[pl.BlockSpec((tm, W), lambda i: (i, 0))] * 3 + [pl.BlockSpec((tm, LANES), lambda i: (i, 0))],
        out_shape=[jax.ShapeDtypeStruct((ntot, W), F32)] * 3 + [jax.ShapeDtypeStruct((ntot, LANES), F32)],
        scratch_shapes=[pltpu.VMEM((tm + 2 * HALO, 3 * W), F32)],
        compiler_params=_params("parallel"),
        name="gdn_prep",
    )(p, p, p, p, conv_w, a_log, dt_bias)


def kernel(x, c, ctx, c_ctx, w_mod, b_mod, w_in, mla_q_norm, mla_kv_norm, mla_w_uq, mla_w_ukv, rwkv_mu, rwkv_w0, rwkv_w2, rwkv_a0, rwkv_a2, rwkv_g2, rwkv_k_k, rwkv_k_a, rwkv_r_k, rwkv_gn_g, rwkv_gn_b, gdn_conv, gdn_a_log, gdn_dt_bias, gdn_norm, w_out, ln1_g, ln1_b, ffn_w_gate, ffn_w_up, ffn_w_down, ln2_g, ln2_b):
    B, T, D = x.shape
    CTX = ctx.shape[1]
    L = w_mod.shape[0]
    n_lat, n_ctx = B * T, B * CTX
    ntot = n_lat + n_ctx
    alpha = (2.0 * L) ** 0.25
    tm = 256
    tm_big = 512 if (n_lat % 512 == 0 and n_ctx % 512 == 0 and T % 512 == 0) else tm
    tf = 512
    assert T % tm == 0 and n_ctx % tm == 0 and B + 1 <= 8

    cvec = jnp.concatenate([c, c_ctx[None], jnp.zeros((8 - B - 1, D), F32)], axis=0)
    mod_all = _modulation_all(cvec, w_mod, b_mod).reshape(L, 8, 6, D)[:, :B + 1]
    mod_all = jnp.pad(mod_all, ((0, 0), (0, 0), (0, 2), (0, 0)))

    in_idx = _in_proj_columns()
    w_in_p = _take_cols(w_in, in_idx).astype(BF16)
    uq_idx = _uq_columns()
    w_uq_p = jnp.take(mla_w_uq, jnp.asarray(uq_idx, jnp.int32), axis=-1).astype(BF16)
    w_ukv_b = mla_w_ukv.astype(BF16)
    w_out_b = w_out.astype(BF16)
    wg_b, wu_b, wd_b = ffn_w_gate.astype(BF16), ffn_w_up.astype(BF16), ffn_w_down.astype(BF16)
    cs1, cs2 = _rope_tables(T, tm)

    W = RWKV_WIDTH
    mu_rkv = rwkv_mu[:, None, :3 * W]
    mu_lg = jnp.pad(rwkv_mu[:, None, 3 * W:], ((0, 0), (0, 0), (0, 2 * LANES - (rwkv_mu.shape[1] - 3 * W))))
    lw, la = RWKV_LORA_W, RWKV_LORA_A
    w2p = jnp.zeros((L, 2, LANES, W), F32)
    a2p = jnp.zeros((L, 2, LANES, W), F32)
    for d in range(2):
        w2p = w2p.at[:, d, d * lw:(d + 1) * lw].set(rwkv_w2[:, d])
        a2p = a2p.at[:, d, 2 * lw + d * la:2 * lw + (d + 1) * la].set(rwkv_a2[:, d])
    w2p, a2p = w2p.astype(BF16), a2p.astype(BF16)
    g2p = jnp.pad(rwkv_g2, ((0, 0), (0, LANES - RWKV_LORA_G), (0, 0))).astype(BF16)
    lane_head = np.arange(W) // RWKV_HEAD
    seg = jnp.asarray(lane_head[:, None] == lane_head[None, :], BF16)
    conv_p = jnp.pad(gdn_conv, ((0, 0), (0, 8 - GDN_CONV), (0, 0)))
    nh2 = 2 * GDN_HEADS
    alog_p = jnp.pad(gdn_a_log.reshape(L, 1, nh2), ((0, 0), (0, 0), (0, LANES - nh2)))
    dt_p = jnp.pad(gdn_dt_bias.reshape(L, 1, nh2), ((0, 0), (0, 0), (0, LANES - nh2)))
    r_k = rwkv_r_k.reshape(L, 1, W)
    gdn_norm_t = jnp.tile(gdn_norm, (1, GDN_HEADS))[:, None, :]

    hs = (x.reshape(n_lat, D), ctx.reshape(n_ctx, D))
    for l in range(L):
        last = l == L - 1
        mod = mod_all[l]
        p = _in_proj(hs, mod, w_in_p[l], n_lat, T, B, tm_big)

        qh, kh, vh = _mla_proj(p, mla_q_norm[l][None], mla_kv_norm[l][None], w_uq_p[l], w_ukv_b[l], cs1, cs2,
                               n_lat, T, tm)
        a_mix = _attention(qh, kh, vh, B, T, CTX, with_ctx=not last)

        r, v, a, lw0, lw1, kd0, kd1, b0, b1, g = _rwkv_prep(
            p, mu_rkv[l], mu_lg[l], rwkv_w0[l], rwkv_a0[l], w2p[l], a2p[l], g2p[l],
            rwkv_k_k[l][None], rwkv_k_a[l][None], seg, n_lat, T, CTX, tm)
        yf, yb = _rwkv_scan(r, v, a, lw0, lw1, kd0, kd1, b0, b1, B, T, CTX)

        gq, gk, gv, gb = _gdn_prep(p, conv_p[l], alog_p[l], dt_p[l], n_lat, T, CTX, tm)
        of, ob = _gdn_scan(gq, gk, gv, gb, B, T, CTX)

        n_rows = n_lat if last else ntot
        h1 = _out_proj(hs, a_mix, (yf, yb, r, v, kd0, kd1, g), (of, ob), p, r_k[l], rwkv_gn_g[l][None],
                       rwkv_gn_b[l][None], gdn_norm_t[l], seg, w_out_b[l], mod, ln1_g[l][None], ln1_b[l][None],
                       n_rows, n_lat, T, B, tm, alpha)
        hs = (_ffn(h1, mod, wg_b[l], wu_b[l], wd_b[l], ln2_g[l][None], ln2_b[l][None],
                   n_rows, n_lat, T, B, tm_big, tf, alpha),)
    return hs[0][:n_lat].reshape(B, T, D)
```

```python
import functools
import math

import numpy as np
import jax
import jax.numpy as jnp
from jax import lax
from jax.experimental import pallas as pl
from jax.experimental.pallas import tpu as pltpu

F32 = jnp.float32
BF16 = jnp.bfloat16
HIGHEST = lax.Precision.HIGHEST

LANES = 128
SUBLANES = 8
GRID_W = 64
ROPE_THETA = 10000.0

MLA_HEADS = 8
MLA_RANK = 512
MLA_NOPE = 128
MLA_ROPE = 64
MLA_V = 128
MLA_QK_PAD = 256
ATTN_SCALE = (MLA_NOPE + MLA_ROPE) ** -0.5
Q_SCALE = ATTN_SCALE * math.log2(math.e)
ATTN_TQ = 256
ATTN_TK = 2048
ATTN_HEADS_PER_STEP = 4

RWKV_HEADS = 8
RWKV_HEAD = 64
RWKV_WIDTH = RWKV_HEADS * RWKV_HEAD
RWKV_LORA_W = 32
RWKV_LORA_A = 32
RWKV_LORA_G = 96
RWKV_GN_EPS = 64e-5
RWKV_CHUNK = 64
RWKV_BLOCK = 128
RWKV_PREP_DTYPES = (BF16, BF16, BF16, F32, F32, BF16, BF16, BF16, BF16, F32)

GDN_HEADS = 4
GDN_HEAD = 128
GDN_WIDTH = GDN_HEADS * GDN_HEAD
GDN_CONV = 5
GDN_CHUNK = 128
GDN_BLOCK = 256

COL_RKV = 0
COL_GQKV = 1536
COL_Z = 3072
COL_CQ = 3584
COL_CKV = 4096
COL_LORA = 4608
COL_KR = 4864
COL_GAB = 4992
IN_COLS_PAD = 5120
HALO = 8
VMEM_LIMIT = 56 * 1024 * 1024


def _dot(a, b):
    return jnp.dot(a, b, preferred_element_type=F32)


def _dot_nt(a, b):
    return lax.dot_general(a, b, (((1,), (1,)), ((), ())), preferred_element_type=F32)


def _dot_tn(a, b):
    return lax.dot_general(a, b, (((0,), (0,)), ((), ())), preferred_element_type=F32)


def _seg_sum(x, seg):
    return _dot(x.astype(BF16), seg)


def _dot_split3(ones_bf16, x):
    t1 = x.astype(BF16)
    r1 = x - t1.astype(F32)
    t2 = r1.astype(BF16)
    t3 = (r1 - t2.astype(F32)).astype(BF16)
    return _dot(ones_bf16, t1) + _dot(ones_bf16, t2) + _dot(ones_bf16, t3)


def _layer_norm(z, g, b):
    mu = jnp.mean(z, axis=-1, keepdims=True)
    zc = z - mu
    var = jnp.mean(zc * zc, axis=-1, keepdims=True)
    return zc * lax.rsqrt(var + 1e-5) * g + b


def _params(*sem):
    return pltpu.CompilerParams(dimension_semantics=sem, vmem_limit_bytes=VMEM_LIMIT)


def _mod_kernel(c_ref, w_ref, b_ref, o_ref):
    cv = c_ref[...]
    s = cv * jax.nn.sigmoid(cv)
    o_ref[0] = _dot(s.astype(BF16), w_ref[0].astype(BF16)) + b_ref[0]


def _modulation_all(cvec, w_mod, b_mod):
    L, D, N = w_mod.shape
    tn = 1024 if N % 1024 == 0 else 512
    return pl.pallas_call(
        _mod_kernel,
        grid=(L, N // tn),
        in_specs=[pl.BlockSpec((8, D), lambda l, j: (0, 0)),
                  pl.BlockSpec((1, D, tn), lambda l, j: (l, 0, j)),
                  pl.BlockSpec((1, 1, tn), lambda l, j: (l, 0, j))],
        out_specs=pl.BlockSpec((1, 8, tn), lambda l, j: (l, 0, j)),
        out_shape=jax.ShapeDtypeStruct((L, 8, N), F32),
        compiler_params=_params("parallel", "parallel"),
        name="adaln_modulation",
    )(cvec, w_mod, b_mod.reshape(L, 1, N))


def _stream_specs(hs, tm, n_lat, row_tile):
    D = hs[0].shape[1]
    if len(hs) == 1:
        return [pl.BlockSpec((tm, D), lambda *g: (row_tile(*g), 0))]
    nl = n_lat // tm
    return [pl.BlockSpec((tm, D), lambda *g: (jnp.minimum(row_tile(*g), nl - 1), 0)),
            pl.BlockSpec((tm, D), lambda *g: (jnp.maximum(row_tile(*g) - nl, 0), 0))]


def _load_stream(x_refs, i, n_lat_tiles):
    if len(x_refs) == 1:
        return x_refs[0][...]
    return jnp.where(i < n_lat_tiles, x_refs[0][...], x_refs[1][...])


def _inproj_kernel(*refs, n_src, n_lat_tiles):
    mod_ref, w_ref, o_ref = refs[n_src:]
    m = mod_ref[0]
    xm = _load_stream(refs[:n_src], pl.program_id(1), n_lat_tiles) * (1.0 + m[1:2]) + m[0:1]
    o_ref[...] = _dot(xm.astype(BF16), w_ref[...])


def _in_proj(hs, mod, w, n_lat, t_len, n_batch, tm):
    ntot = sum(h.shape[0] for h in hs)
    D = hs[0].shape[1]
    ncol = w.shape[1]
    tn = ncol // 2

    def mod_row(i):
        return jnp.where(i * tm < n_lat, (i * tm) // t_len, n_batch)

    return pl.pallas_call(
        functools.partial(_inproj_kernel, n_src=len(hs), n_lat_tiles=n_lat // tm),
        grid=(ncol // tn, ntot // tm),
        in_specs=_stream_specs(hs, tm, n_lat, lambda j, i: i)
                 + [pl.BlockSpec((1, 8, D), lambda j, i: (mod_row(i), 0, 0)),
                    pl.BlockSpec((D, tn), lambda j, i: (0, j))],
        out_specs=pl.BlockSpec((tm, tn), lambda j, i: (i, j)),
        out_shape=jax.ShapeDtypeStruct((ntot, ncol), F32),
        compiler_params=_params("parallel", "parallel"),
        name="in_proj",
    )(*hs, mod, w)


def _mla_proj_kernel(cq_ref, ckv_ref, kr_ref, qn_ref, kn_ref, wq_ref, wkv_ref, cs1_ref, cs2_ref,
                     q_ref, k_ref, v_ref):
    cs1 = cs1_ref[...]
    cs2 = cs2_ref[...]

    def rms(x, g):
        return (x * lax.rsqrt(jnp.mean(x * x, axis=-1, keepdims=True) + 1e-6) * g).astype(BF16)

    def rope(g):
        return g * cs1 + pltpu.roll(g, MLA_ROPE, 1) * cs2

    oq = _dot(rms(cq_ref[...], qn_ref[...]), wq_ref[...])
    okv = _dot(rms(ckv_ref[...], kn_ref[...]), wkv_ref[...])
    krot = rope(kr_ref[...])
    for hd in range(MLA_HEADS):
        c0 = hd * MLA_QK_PAD
        qn = oq[:, c0:c0 + MLA_NOPE]
        qr = rope(oq[:, c0 + MLA_NOPE:c0 + MLA_QK_PAD])
        q_ref[hd] = (jnp.concatenate([qn, qr], axis=1) * Q_SCALE).astype(BF16)
        k_ref[hd] = jnp.concatenate([okv[:, c0:c0 + MLA_NOPE], krot], axis=1).astype(BF16)
        v_ref[hd] = okv[:, c0 + MLA_NOPE:c0 + MLA_QK_PAD].T.astype(BF16)


def _mla_proj(p, q_norm, kv_norm, wq, wkv, cs1, cs2, n_lat, t_len, tm):
    ntot = p.shape[0]
    H = MLA_HEADS
    npos = t_len // tm

    def pos(i):
        return jnp.where(i * tm < n_lat, i % npos, npos)

    return pl.pallas_call(
        _mla_proj_kernel,
        grid=(ntot // tm,),
        in_specs=[pl.BlockSpec((tm, MLA_RANK), lambda i: (i, COL_CQ // MLA_RANK)),
                  pl.BlockSpec((tm, MLA_RANK), lambda i: (i, COL_CKV // MLA_RANK)),
                  pl.BlockSpec((tm, LANES), lambda i: (i, COL_KR // LANES)),
                  pl.BlockSpec((1, MLA_RANK), lambda i: (0, 0)),
                  pl.BlockSpec((1, MLA_RANK), lambda i: (0, 0)),
                  pl.BlockSpec(wq.shape, lambda i: (0, 0)),
                  pl.BlockSpec(wkv.shape, lambda i: (0, 0)),
                  pl.BlockSpec((tm, LANES), lambda i: (pos(i), 0)),
                  pl.BlockSpec((tm, LANES), lambda i: (pos(i), 0))],
        out_specs=[pl.BlockSpec((H, tm, MLA_QK_PAD), lambda i: (0, i, 0)),
                   pl.BlockSpec((H, tm, MLA_QK_PAD), lambda i: (0, i, 0)),
                   pl.BlockSpec((H, MLA_V, tm), lambda i: (0, 0, i))],
        out_shape=[jax.ShapeDtypeStruct((H, ntot, MLA_QK_PAD), BF16),
                   jax.ShapeDtypeStruct((H, ntot, MLA_QK_PAD), BF16),
                   jax.ShapeDtypeStruct((H, MLA_V, ntot), BF16)],
        compiler_params=_params("parallel"),
        name="mla_proj",
    )(p, p, p, q_norm, kv_norm, wq, wkv, cs1, cs2)


def _softmax_pv(scores, values):
    def rows(x):
        return x.reshape(x.shape[0] // SUBLANES, SUBLANES, x.shape[1])

    m = functools.reduce(jnp.maximum, [jnp.max(rows(s), axis=0) for s in scores])
    m = jnp.max(m, axis=0, keepdims=True)
    ps = [jnp.exp2(s - m) for s in scores]
    l = functools.reduce(jnp.add, [jnp.sum(rows(p), axis=0) for p in ps])
    l = jnp.sum(l, axis=0, keepdims=True)
    acc = functools.reduce(jnp.add, [_dot(v, p.astype(BF16)) for p, v in zip(ps, values)])
    return (acc / l).T


def _attn_kernel(q_ref, kl_ref, kc_ref, vl_ref, vc_ref, o_ref, *, nq_lat, tk):
    qi = pl.program_id(2)
    n_chunks = kl_ref.shape[1] // tk
    heads = range(ATTN_HEADS_PER_STEP)

    @pl.when(qi < nq_lat)
    def _():
        scores, values = [], []
        for hd in heads:
            q = q_ref[hd]
            sc, va = [_dot_nt(kc_ref[hd], q)], [vc_ref[hd]]
            for c in range(n_chunks):
                sc.append(_dot_nt(kl_ref[hd, c * tk:(c + 1) * tk, :], q))
                va.append(vl_ref[hd, :, c * tk:(c + 1) * tk])
            scores.append(sc)
            values.append(va)
        for hd in heads:
            o_ref[:, hd * MLA_V:(hd + 1) * MLA_V] = _softmax_pv(scores[hd], values[hd]).astype(o_ref.dtype)

    @pl.when(qi >= nq_lat)
    def _():
        for hd in heads:
            o = _softmax_pv([_dot_nt(kc_ref[hd], q_ref[hd])], [vc_ref[hd]])
            o_ref[:, hd * MLA_V:(hd + 1) * MLA_V] = o.astype(o_ref.dtype)


def _attention(qh, kh, vh, n_batch, t_len, ctx_len, with_ctx):
    H, ntot, _ = qh.shape
    n_lat = n_batch * t_len
    tq = ATTN_TQ
    hp = ATTN_HEADS_PER_STEP
    tk = min(ATTN_TK, t_len)
    assert ctx_len == tq and t_len % tk == 0 and H % hp == 0
    nq_lat = t_len // tq
    nq = nq_lat + (1 if with_ctx else 0)
    ctx_blk0 = n_lat // ctx_len

    def qrow(b, qi):
        return jnp.where(qi < nq_lat, b * nq_lat + qi, n_lat // tq + b)

    kern = functools.partial(_attn_kernel, nq_lat=nq_lat, tk=tk)
    return pl.pallas_call(
        kern,
        grid=(n_batch, H // hp, nq),
        in_specs=[pl.BlockSpec((hp, tq, MLA_QK_PAD), lambda b, h, qi: (h, qrow(b, qi), 0)),
                  pl.BlockSpec((hp, t_len, MLA_QK_PAD), lambda b, h, qi: (h, b, 0)),
                  pl.BlockSpec((hp, ctx_len, MLA_QK_PAD), lambda b, h, qi: (h, ctx_blk0 + b, 0)),
                  pl.BlockSpec((hp, MLA_V, t_len), lambda b, h, qi: (h, 0, b)),
                  pl.BlockSpec((hp, MLA_V, ctx_len), lambda b, h, qi: (h, 0, ctx_blk0 + b))],
        out_specs=pl.BlockSpec((tq, hp * MLA_V), lambda b, h, qi: (qrow(b, qi), h)),
        out_shape=jax.ShapeDtypeStruct((ntot, H * MLA_V), BF16),
        compiler_params=_params("parallel", "parallel", "arbitrary"),
        name="mla_attention",
    )(qh, kh, kh, vh, vh)


def _tri(n, reverse, transpose=False):
    i = lax.broadcasted_iota(jnp.int32, (n, n), 0)
    j = lax.broadcasted_iota(jnp.int32, (n, n), 1)
    if transpose:
        i, j = j, i
    return ((j >= i) if reverse else (j <= i)).astype(F32)


def _interleave(gens):
    out = [None] * len(gens)
    live = set(range(len(gens)))
    while live:
        for c in sorted(live):
            try:
                next(gens[c])
            except StopIteration as stop:
                out[c] = stop.value
                live.discard(c)
    return out


def _tri_inverse_consts(n, size):
    sub = SUBLANES
    ii = lax.broadcasted_iota(jnp.int32, (n, n), 0)
    jj = lax.broadcasted_iota(jnp.int32, (n, n), 1)
    diag = (ii // sub) == (jj // sub)
    lane = lax.broadcasted_iota(jnp.int32, (sub, n), 1) % sub
    row = lax.broadcasted_iota(jnp.int32, (sub, n), 0)
    offs = []
    s = sub
    while s < size:
        offs.append(((ii // (2 * s)) == (jj // (2 * s))) & ((ii // s) != (jj // s)))
        s *= 2
    return diag, lane, row, offs


def _diag_block_inverse(nmat, rev, consts):
    diag, lane, row, _ = consts
    n = nmat.shape[0]
    sub = SUBLANES
    nc = jnp.sum(jnp.where(diag, nmat, 0.0).reshape(n // sub, sub, n), axis=0)
    cols = []
    for j in range(sub):
        one = jnp.where(lane == j, nc, 0.0)
        up, dn = one, one
        for sh in (1, 2, 4):
            up = up + pltpu.roll(up, sh, 1)
            dn = dn + pltpu.roll(dn, n - sh, 1)
        cols.append(jnp.where(lane >= j, up, dn))
    x = (lane == row).astype(F32)
    for j in (range(sub - 1, 0, -1) if rev else range(sub - 1)):
        x = x + cols[j] * x[j:j + 1, :]
    return jnp.where(diag, jnp.broadcast_to(x[None], (n // sub, sub, n)).reshape(n, n), 0.0)


def _unit_tri_inverse_steps(nmat, rev, consts):
    x = _diag_block_inverse(nmat, rev, consts)
    yield
    for off in consts[3]:
        xb = x.astype(BF16)
        t1 = _dot(xb, jnp.where(off, nmat, 0.0).astype(BF16)).astype(BF16)
        yield
        x = x + _dot(t1, xb)
        yield
    return x


def _unit_tri_inverses(nmats, revs, size):
    consts = _tri_inverse_consts(nmats[0].shape[0], size)
    return _interleave([_unit_tri_inverse_steps(nm, rev, consts) for nm, rev in zip(nmats, revs)])


def _rwkv_chunks(chains, revs):
    C = RWKV_CHUNK
    n2 = 2 * C
    tri = {rev: _tri(C, rev).astype(BF16) for rev in set(revs)}
    lane = lax.broadcasted_iota(jnp.int32, (C, LANES), 1)
    m0 = (lane < RWKV_HEAD).astype(F32)
    m1 = 1.0 - m0
    ii = lax.broadcasted_iota(jnp.int32, (n2, n2), 0)
    jj = lax.broadcasted_iota(jnp.int32, (n2, n2), 1)
    same = (ii // C) == (jj // C)
    il, jl = ii % C, jj % C
    strict = {rev: same & ((jl > il) if rev else (jl < il)) for rev in set(revs)}
    incl = {rev: same & ((jl >= il) if rev else (jl <= il)) for rev in set(revs)}
    inv_consts = _tri_inverse_consts(n2, C)

    def ex(x):
        return jnp.concatenate([x * m0, x * m1], axis=0)

    def chain(r, v, a, lw, kd, bd, rev):
        cum = _dot_split3(tri[rev], lw)
        tot = jnp.sum(lw, axis=0, keepdims=True)
        yield
        e_neg = jnp.exp(-cum)
        e_rem = jnp.exp(tot - cum)
        at, rt = ex(a * jnp.exp(cum - lw)), ex(r * jnp.exp(cum))
        bt, kt = ex(bd * e_neg), ex(kd * e_neg)
        bh, kh = ex(bd * e_rem).astype(BF16), ex(kd * e_rem).astype(BF16)
        ve = ex(v).astype(BF16)
        yield
        aa = _dot_nt(jnp.concatenate([at, rt], axis=0).astype(BF16), jnp.concatenate([bt, kt], axis=0).astype(BF16))
        yield
        a_ab = jnp.where(strict[rev], aa[:n2, :n2], 0.0)
        a_ak = jnp.where(strict[rev], aa[:n2, n2:], 0.0).astype(BF16)
        a_rb = jnp.where(incl[rev], aa[n2:, :n2], 0.0).astype(BF16)
        a_rk = jnp.where(incl[rev], aa[n2:, n2:], 0.0).astype(BF16)
        akv = _dot(a_ak, ve)
        y0b = _dot(a_rk, ve)
        yield
        tinv = yield from _unit_tri_inverse_steps(a_ab, rev, inv_consts)
        zb = _dot(tinv.astype(BF16), jnp.concatenate([at, akv], axis=1).astype(BF16)).astype(BF16)
        yield
        w2 = _dot(a_rb, zb)
        gmt = _dot_tn(zb[:, :LANES], bh).astype(BF16)
        s0t = _dot_tn(jnp.concatenate([zb[:, LANES:], ve], axis=0), jnp.concatenate([bh, kh], axis=0))
        yield
        rhat = (rt + w2[:, :LANES]).astype(BF16)
        y0 = w2[:, LANES:] + y0b
        return rhat, y0, gmt, s0t, jnp.exp(tot)

    return _interleave([chain(*c, rev) for c, rev in zip(chains, revs)])


def _rwkv_apply(pre, states):
    C = RWKV_CHUNK
    sbs = [s.astype(BF16) for s in states]
    yes = [_dot_nt(rh, sb) + y0 for (rh, y0, _, _, _), sb in zip(pre, sbs)]
    ys = [ye[:C] + ye[C:] for ye in yes]
    new_states = [s * dec + _dot(sb, g) + s0 for (_, _, g, s0, dec), s, sb in zip(pre, states, sbs)]
    return ys, new_states


def _rwkv_scan_kernel(rf, rb, vf, vb, af, ab, lw0, lw1, kd0, kd1, b0, b1, yf_ref, yb_ref, s_ref):
    @pl.when(pl.program_id(1) == 0)
    def _():
        s_ref[...] = jnp.zeros_like(s_ref)

    C = RWKV_CHUNK
    nsub = rf.shape[0] // C
    dirs = ((rf, vf, af, lw0, kd0, b0, yf_ref), (rb, vb, ab, lw1, kd1, b1, yb_ref))
    npair = RWKV_WIDTH // LANES
    chains, revs, where = [], [], []
    for d, (r_, v_, a_, lw_, kd_, b_, y_) in enumerate(dirs):
        for j in range(npair):
            sl = slice(j * LANES, (j + 1) * LANES)
            for c in range(nsub):
                rows = slice(c * C, (c + 1) * C)
                chains.append(tuple(ref[rows, sl].astype(F32) for ref in (r_, v_, a_, lw_, kd_, b_)))
                revs.append(d == 1)
                where.append((d, j, c))
    pre = dict(zip(where, _rwkv_chunks(chains, revs)))
    keys = [(d, j) for d in range(2) for j in range(npair)]
    states = [s_ref[d, j] for d, j in keys]
    for step in range(nsub):
        sub = [step if d == 0 else nsub - 1 - step for d, _ in keys]
        ys, states = _rwkv_apply([pre[(d, j, c)] for (d, j), c in zip(keys, sub)], states)
        for (d, j), c, y in zip(keys, sub, ys):
            dirs[d][6][c * C:(c + 1) * C, j * LANES:(j + 1) * LANES] = y
    for (d, j), s_new in zip(keys, states):
        s_ref[d, j] = s_new


def _chunk_maps(n_batch, t_len, ctx_len, chunk):
    nctx, nlat = ctx_len // chunk, t_len // chunk
    base = n_batch * nlat

    def fwd(b, i):
        return jnp.where(i < nctx, base + b * nctx + i, b * nlat + i - nctx)

    def bwd(b, i):
        return jnp.where(i < nctx, base + b * nctx + (nctx - 1 - i), b * nlat + (nlat - 1 - (i - nctx)))

    return fwd, bwd, nctx + nlat


def _rwkv_scan(r, v, a, lw0, lw1, kd0, kd1, b0, b1, n_batch, t_len, ctx_len):
    ntot, W = r.shape
    C = RWKV_BLOCK
    fwd, bwd, nsteps = _chunk_maps(n_batch, t_len, ctx_len, C)
    fs = pl.BlockSpec((C, W), lambda b, i: (fwd(b, i), 0))
    bs = pl.BlockSpec((C, W), lambda b, i: (bwd(b, i), 0))
    return pl.pallas_call(
        _rwkv_scan_kernel,
        grid=(n_batch, nsteps),
        in_specs=[fs, bs, fs, bs, fs, bs, fs, bs, fs, bs, fs, bs],
        out_specs=[fs, bs],
        out_shape=[jax.ShapeDtypeStruct((ntot, W), F32)] * 2,
        scratch_shapes=[pltpu.VMEM((2, W // LANES, LANES, LANES), F32)],
        compiler_params=_params("parallel", "arbitrary"),
        name="rwkv7_scan",
    )(r, r, v, v, a, a, lw0, lw1, kd0, kd1, b0, b1)


def _gdn_scan_kernel(qf, qb, kf, kb, vf, vb, gf, gb, of_ref, ob_ref, s_ref):
    @pl.when(pl.program_id(1) == 0)
    def _():
        s_ref[...] = jnp.zeros_like(s_ref)

    C = GDN_CHUNK
    DH = GDN_HEAD
    ii = lax.broadcasted_iota(jnp.int32, (C, C), 0)
    jj = lax.broadcasted_iota(jnp.int32, (C, C), 1)
    strict = {False: jj < ii, True: jj > ii}
    incl = {False: jj <= ii, True: jj >= ii}
    nsub = qf.shape[0] // C
    dirs = ((qf, kf, vf, gf, of_ref), (qb, kb, vb, gb, ob_ref))
    qs, ks, vs, gcs, grs, gts, betas, revs, where = [], [], [], [], [], [], [], [], []
    for d, (q_, k_, v_, g_, o_) in enumerate(dirs):
        rev = d == 1
        for c in range(nsub):
            rows = slice(c * C, (c + 1) * C)
            gbv = g_[rows, :]
            gcol = jnp.dot(_tri(C, rev), gbv, precision=HIGHEST, preferred_element_type=F32)
            grow = jnp.dot(gbv.T, _tri(C, rev, transpose=True), precision=HIGHEST, preferred_element_type=F32)
            gtot = jnp.sum(gbv, axis=0, keepdims=True)
            for hd in range(GDN_HEADS):
                sl = slice(hd * DH, (hd + 1) * DH)
                cg = d * GDN_HEADS + hd
                cb = 2 * GDN_HEADS + cg
                qs.append(q_[rows, sl].astype(F32)); ks.append(k_[rows, sl].astype(F32))
                vs.append(v_[rows, sl].astype(F32))
                gcs.append(gcol[:, cg:cg + 1]); grs.append(grow[cg:cg + 1, :]); gts.append(gtot[:, cg:cg + 1])
                betas.append(gbv[:, cb:cb + 1]); revs.append(rev); where.append((d, hd, c))

    decs = [jnp.exp(jnp.where(incl[rev], gc - gr, -1e30)) for gc, gr, rev in zip(gcs, grs, revs)]
    kbetas = [k * b for k, b in zip(ks, betas)]
    qks = [_dot_nt(jnp.concatenate([kb_, q], axis=0).astype(BF16), k.astype(BF16)) for kb_, q, k in zip(kbetas, qs, ks)]
    nmats = [jnp.where(strict[rev], -qk[:C] * dec, 0.0) for qk, dec, rev in zip(qks, decs, revs)]
    aints = [jnp.where(incl[rev], qk[C:] * dec, 0.0).astype(BF16) for qk, dec, rev in zip(qks, decs, revs)]
    egs = [jnp.exp(gc) for gc in gcs]
    rhss = [jnp.concatenate([v * b, kb_ * eg], axis=1).astype(BF16) for v, b, kb_, eg in zip(vs, betas, kbetas, egs)]
    khats = [(k * jnp.exp(gt - gc)).astype(BF16) for k, gt, gc in zip(ks, gts, gcs)]
    tinvs = _unit_tri_inverses(nmats, revs, C)
    sols = [_dot(ti.astype(BF16), rhs).astype(BF16) for ti, rhs in zip(tinvs, rhss)]
    qos = [_dot(ai, sol) for ai, sol in zip(aints, sols)]
    qhats = [(q * eg - qo[:, DH:]).astype(BF16) for q, eg, qo in zip(qs, egs, qos)]
    gss = [_dot_tn(kh, sol) for kh, sol in zip(khats, sols)]
    pre = dict(zip(where, zip(qhats, qos, gss, [jnp.exp(gt) for gt in gts])))

    keys = [(d, hd) for d in range(2) for hd in range(GDN_HEADS)]
    states = [s_ref[d, hd] for d, hd in keys]
    for step in range(nsub):
        sub = [step if d == 0 else nsub - 1 - step for d, _ in keys]
        cur = [pre[(d, hd, c)] for (d, hd), c in zip(keys, sub)]
        sbs = [s.astype(BF16) for s in states]
        os_ = [_dot(qh, sb) + qo[:, :DH] for (qh, qo, _, _), sb in zip(cur, sbs)]
        states = [s * dec - _dot(gs[:, DH:].astype(BF16), sb) + gs[:, :DH]
                  for (_, _, gs, dec), s, sb in zip(cur, states, sbs)]
        for (d, hd), c, o in zip(keys, sub, os_):
            dirs[d][4][c * C:(c + 1) * C, hd * DH:(hd + 1) * DH] = o
    for (d, hd), s_new in zip(keys, states):
        s_ref[d, hd] = s_new


def _gdn_scan(q, k, v, gb, n_batch, t_len, ctx_len):
    ntot, W = q.shape
    C = GDN_BLOCK
    fwd, bwd, nsteps = _chunk_maps(n_batch, t_len, ctx_len, C)
    fs = pl.BlockSpec((C, W), lambda b, i: (fwd(b, i), 0))
    bs = pl.BlockSpec((C, W), lambda b, i: (bwd(b, i), 0))
    gfs = pl.BlockSpec((C, LANES), lambda b, i: (fwd(b, i), 0))
    gbs = pl.BlockSpec((C, LANES), lambda b, i: (bwd(b, i), 0))
    return pl.pallas_call(
        _gdn_scan_kernel,
        grid=(n_batch, nsteps),
        in_specs=[fs, bs, fs, bs, fs, bs, gfs, gbs],
        out_specs=[fs, bs],
        out_shape=[jax.ShapeDtypeStruct((ntot, W), F32)] * 2,
        scratch_shapes=[pltpu.VMEM((2, GDN_HEADS, GDN_HEAD, GDN_HEAD), F32)],
        compiler_params=_params("parallel", "arbitrary"),
        name="gdn_scan",
    )(q, q, k, k, v, v, gb, gb)


def _outproj_kernel(*refs, alpha, n_src, n_lat_tiles):
    (a_ref, yf_ref, yb_ref, r_ref, v_ref, kd0_ref, kd1_ref, g_ref, of_ref, ob_ref, z_ref,
     rk_ref, gng_ref, gnb_ref, gdn_ref, seg_ref, w_ref, mod_ref, lng_ref, lnb_ref, o_ref) = refs[n_src:]
    seg = seg_ref[...]
    y = yf_ref[...] + yb_ref[...]
    yc = y - _seg_sum(y, seg) * (1.0 / RWKV_HEAD)
    var = _seg_sum(yc * yc, seg) * (1.0 / RWKV_HEAD)
    yn = yc * lax.rsqrt(var + RWKV_GN_EPS) * gng_ref[...] + gnb_ref[...]
    kd = kd0_ref[...].astype(F32) + kd1_ref[...].astype(F32)
    bonus = _seg_sum(r_ref[...].astype(F32) * kd * rk_ref[...], seg) * v_ref[...].astype(F32)
    b_mix = ((yn + bonus) * g_ref[...]).astype(BF16)
    o = of_ref[...] + ob_ref[...]
    z = z_ref[...]
    parts = []
    for hd in range(GDN_HEADS):
        oh = o[:, hd * GDN_HEAD:(hd + 1) * GDN_HEAD]
        parts.append(oh * lax.rsqrt(jnp.mean(oh * oh, axis=-1, keepdims=True) + 1e-6))
    c_mix = (jnp.concatenate(parts, axis=1) * gdn_ref[...] * (z * jax.nn.sigmoid(z))).astype(BF16)

    na, nb = a_ref.shape[1], RWKV_WIDTH
    acc = _dot(a_ref[...], w_ref[0:na, :])
    acc += _dot(b_mix, w_ref[na:na + nb, :])
    acc += _dot(c_mix, w_ref[na + nb:, :])
    gate = mod_ref[0][2:3]
    x = _load_stream(refs[:n_src], pl.program_id(0), n_lat_tiles)
    o_ref[...] = _layer_norm(alpha * x + gate * acc, lng_ref[...], lnb_ref[...])


def _out_proj(hs, a, rw, gd, p, r_k, gn_g, gn_b, gdn_norm, seg, w, mod, ln_g, ln_b,
              n_rows, n_lat, t_len, n_batch, tm, alpha):
    D = hs[0].shape[1]
    W = RWKV_WIDTH

    def mod_row(i):
        return jnp.where(i * tm < n_lat, (i * tm) // t_len, n_batch)

    tok = pl.BlockSpec((tm, W), lambda i: (i, 0))
    vec = pl.BlockSpec((1, W), lambda i: (0, 0))
    return pl.pallas_call(
        functools.partial(_outproj_kernel, alpha=alpha, n_src=len(hs), n_lat_tiles=n_lat // tm),
        grid=(n_rows // tm,),
        in_specs=_stream_specs(hs, tm, n_lat, lambda i: i)
                 + [pl.BlockSpec((tm, a.shape[1]), lambda i: (i, 0))]
                 + [tok] * 9
                 + [pl.BlockSpec((tm, W), lambda i: (i, COL_Z // W)),
                    vec, vec, vec, vec,
                    pl.BlockSpec(seg.shape, lambda i: (0, 0)),
                    pl.BlockSpec(w.shape, lambda i: (0, 0)),
                    pl.BlockSpec((1, 8, D), lambda i: (mod_row(i), 0, 0)),
                    pl.BlockSpec((1, D), lambda i: (0, 0)),
                    pl.BlockSpec((1, D), lambda i: (0, 0))],
        out_specs=pl.BlockSpec((tm, D), lambda i: (i, 0)),
        out_shape=jax.ShapeDtypeStruct((n_rows, D), F32),
        compiler_params=_params("parallel"),
        name="out_proj_ln",
    )(*hs, a, *rw, *gd, p, r_k, gn_g, gn_b, gdn_norm, seg, w, mod, ln_g, ln_b)


def _ffn_kernel(x_ref, mod_ref, wg_ref, wu_ref, wd_ref, g_ref, beta_ref, o_ref, xm_ref, *, alpha):
    f = pl.program_id(1)

    @pl.when(f == 0)
    def _():
        m = mod_ref[0]
        xm_ref[...] = (x_ref[...] * (1.0 + m[4:5]) + m[3:4]).astype(BF16)
        o_ref[...] = jnp.zeros_like(o_ref)

    xm = xm_ref[...]
    gate = _dot(xm, wg_ref[...])
    up = _dot(xm, wu_ref[...])
    act = gate * jax.nn.sigmoid(gate) * up
    o_ref[...] += _dot(act.astype(BF16), wd_ref[...])

    @pl.when(f == pl.num_programs(1) - 1)
    def _():
        m = mod_ref[0]
        o_ref[...] = _layer_norm(alpha * x_ref[...] + m[5:6] * o_ref[...], g_ref[...], beta_ref[...])


def _ffn(h, mod, wg, wu, wd, ln_g, ln_b, n_rows, n_lat, t_len, n_batch, tm, tf, alpha):
    D = h.shape[1]
    F = wg.shape[1]

    def mod_row(i):
        return jnp.where(i * tm < n_lat, (i * tm) // t_len, n_batch)

    return pl.pallas_call(
        functools.partial(_ffn_kernel, alpha=alpha),
        grid=(n_rows // tm, F // tf),
        in_specs=[pl.BlockSpec((tm, D), lambda i, f: (i, 0)),
                  pl.BlockSpec((1, 8, D), lambda i, f: (mod_row(i), 0, 0)),
                  pl.BlockSpec((D, tf), lambda i, f: (0, f)),
                  pl.BlockSpec((D, tf), lambda i, f: (0, f)),
                  pl.BlockSpec((tf, D), lambda i, f: (f, 0)),
                  pl.BlockSpec((1, D), lambda i, f: (0, 0)),
                  pl.BlockSpec((1, D), lambda i, f: (0, 0))],
        out_specs=pl.BlockSpec((tm, D), lambda i, f: (i, 0)),
        out_shape=jax.ShapeDtypeStruct((n_rows, D), F32),
        scratch_shapes=[pltpu.VMEM((tm, D), BF16)],
        compiler_params=_params("parallel", "arbitrary"),
        name="ffn_ln",
    )(h, mod, wg, wu, wd, ln_g, ln_b)


def _in_proj_columns():
    kr0 = 2 * MLA_RANK
    swap = np.concatenate([np.arange(16, 32), np.arange(0, 16), np.arange(48, 64), np.arange(32, 48)])
    pr0 = kr0 + MLA_ROPE
    n_lora = 2 * RWKV_LORA_W + 2 * RWKV_LORA_A + RWKV_LORA_G
    pg0 = pr0 + 3 * RWKV_WIDTH + n_lora
    nh4 = 4 * GDN_HEADS
    idx = -np.ones(IN_COLS_PAD, np.int64)

    def put(dst, src):
        idx[dst:dst + len(src)] = src

    put(COL_RKV, pr0 + np.arange(3 * RWKV_WIDTH))
    put(COL_GQKV, pg0 + np.arange(3 * GDN_WIDTH))
    put(COL_Z, pg0 + 3 * GDN_WIDTH + np.arange(GDN_WIDTH))
    put(COL_CQ, np.arange(MLA_RANK))
    put(COL_CKV, MLA_RANK + np.arange(MLA_RANK))
    put(COL_LORA, pr0 + 3 * RWKV_WIDTH + np.arange(n_lora))
    put(COL_KR, kr0 + np.arange(MLA_ROPE))
    put(COL_KR + MLA_ROPE, kr0 + swap)
    put(COL_GAB, pg0 + 4 * GDN_WIDTH + np.arange(nh4))
    return idx


def _take_cols(w, idx):
    pieces = []
    i, n = 0, len(idx)
    while i < n:
        j = i + 1
        if idx[i] < 0:
            while j < n and idx[j] < 0:
                j += 1
            pieces.append(jnp.zeros(w.shape[:-1] + (j - i,), w.dtype))
        else:
            while j < n and idx[j] == idx[j - 1] + 1:
                j += 1
            pieces.append(w[..., int(idx[i]):int(idx[i]) + (j - i)])
        i = j
    return jnp.concatenate(pieces, axis=-1)


def _uq_columns():
    swap = np.concatenate([np.arange(16, 32), np.arange(0, 16), np.arange(48, 64), np.arange(32, 48)])
    per = MLA_NOPE + MLA_ROPE
    out = []
    for hd in range(MLA_HEADS):
        out += [hd * per + np.arange(MLA_NOPE), hd * per + MLA_NOPE + np.arange(MLA_ROPE), hd * per + MLA_NOPE + swap]
    return np.concatenate(out)


def _rope_tables(t_len, tm):
    rows = t_len // GRID_W
    row = np.repeat(np.arange(rows), GRID_W).astype(np.float64)
    col = np.tile(np.arange(GRID_W), rows).astype(np.float64)
    n_freq = MLA_ROPE // 4
    inv = ROPE_THETA ** (-np.arange(n_freq, dtype=np.float64) / n_freq)
    ar = row[:, None] * inv
    ac = col[:, None] * inv
    cos = np.concatenate([np.cos(ar), np.cos(ar), np.cos(ac), np.cos(ac)], axis=1)
    sin = np.concatenate([-np.sin(ar), np.sin(ar), -np.sin(ac), np.sin(ac)], axis=1)
    zer = np.zeros((t_len, LANES - MLA_ROPE))
    cs1 = np.concatenate([cos, zer], axis=1)
    cs2 = np.concatenate([sin, zer], axis=1)
    ident = np.concatenate([np.ones((tm, MLA_ROPE)), np.zeros((tm, LANES - MLA_ROPE))], axis=1)
    cs1 = np.concatenate([cs1, ident], axis=0)
    cs2 = np.concatenate([cs2, np.zeros((tm, LANES))], axis=0)
    return jnp.asarray(cs1, F32), jnp.asarray(cs2, F32)


def _fill_halo(ext_ref, cur_ref, prev_ref, next_ref, first, last):
    tm = cur_ref.shape[0]
    ext_ref[0:HALO, :] = jnp.where(first, 0.0, prev_ref[...])
    ext_ref[HALO:HALO + tm, :] = cur_ref[...]
    ext_ref[HALO + tm:, :] = jnp.where(last, 0.0, next_ref[...])


def _seq_edges(tm, n_lat, t_len, ctx_len):
    t0 = pl.program_id(0) * tm
    lat = t0 < n_lat
    first = jnp.where(lat, t0 % t_len == 0, (t0 - n_lat) % ctx_len == 0)
    last = jnp.where(lat, (t0 + tm) % t_len == 0, (t0 - n_lat + tm) % ctx_len == 0)
    return first, last


def _softplus(x):
    return jnp.maximum(x, 0.0) + jnp.log(1.0 + jnp.exp(-jnp.abs(x)))


def _rwkv_prep_kernel(rkv_ref, rkv_p, rkv_n, lg_ref, lg_p, lg_n, mu_rkv, mu_lg, w0_ref, a0_ref, w2_ref, a2_ref,
                      g2_ref, kk_ref, ka_ref, seg_ref,
                      r_o, v_o, a_o, lw0_o, lw1_o, kd0_o, kd1_o, b0_o, b1_o, g_o, ext_rkv, ext_lg,
                      *, n_lat, t_len, ctx_len):
    tm = rkv_ref.shape[0]
    W = RWKV_WIDTH
    first, last = _seq_edges(tm, n_lat, t_len, ctx_len)
    _fill_halo(ext_rkv, rkv_ref, rkv_p, rkv_n, first, last)
    _fill_halo(ext_lg, lg_ref, lg_p, lg_n, first, last)

    def mixed(ext, mu):
        cur = ext[HALO:HALO + tm, :]
        sh = 0.5 * (ext[HALO - 1:HALO - 1 + tm, :] + ext[HALO + 1:HALO + 1 + tm, :])
        return cur + (sh - cur) * mu[...]

    p = mixed(ext_rkv, mu_rkv)
    lg = mixed(ext_lg, mu_lg)
    r, k, v = p[:, :W], p[:, W:2 * W], p[:, 2 * W:]
    lora = lg[:, :LANES]
    tw = jnp.tanh(lora).astype(BF16)
    ab = lora.astype(BF16)
    kkr = k * kk_ref[...]
    kk = kkr * lax.rsqrt(_seg_sum(kkr * kkr, seg_ref[...]) + 1e-12)
    r_o[...] = r.astype(r_o.dtype)
    v_o[...] = v.astype(v_o.dtype)
    a_o[...] = (-kk).astype(a_o.dtype)
    g_o[...] = _dot(jax.nn.sigmoid(lg[:, LANES:]).astype(BF16), g2_ref[...])
    for d, (lw_o, kd_o, b_o) in enumerate(((lw0_o, kd0_o, b0_o), (lw1_o, kd1_o, b1_o))):
        w_log = -_softplus(-(w0_ref[d:d + 1, :] + _dot(tw, w2_ref[d]))) - 0.5
        lw_o[...] = -jnp.exp(w_log)
        a_lr = jax.nn.sigmoid(a0_ref[d:d + 1, :] + _dot(ab, a2_ref[d]))
        kd_o[...] = (k * (1.0 + (a_lr - 1.0) * ka_ref[...])).astype(kd_o.dtype)
        b_o[...] = (kk * a_lr).astype(b_o.dtype)


def _halo_specs(tm, width, col, ntot):
    nb = tm // HALO
    last_blk = ntot // HALO - 1
    return [pl.BlockSpec((tm, width), lambda i: (i, col)),
            pl.BlockSpec((HALO, width), lambda i: (jnp.maximum(i * nb - 1, 0), col)),
            pl.BlockSpec((HALO, width), lambda i: (jnp.minimum((i + 1) * nb, last_blk), col))]


def _rwkv_prep(p, mu_rkv, mu_lg, w0, a0, w2p, a2p, g2p, k_k, k_a, seg, n_lat, t_len, ctx_len, tm):
    ntot = p.shape[0]
    W = RWKV_WIDTH
    full = lambda arr: pl.BlockSpec(arr.shape, lambda i: (0,) * arr.ndim)
    params = (mu_rkv, mu_lg, w0, a0, w2p, a2p, g2p, k_k, k_a, seg)
    return pl.pallas_call(
        functools.partial(_rwkv_prep_kernel, n_lat=n_lat, t_len=t_len, ctx_len=ctx_len),
        grid=(ntot // tm,),
        in_specs=_halo_specs(tm, 3 * W, COL_RKV // (3 * W), ntot) + _halo_specs(tm, 2 * LANES, COL_LORA // (2 * LANES), ntot)
                 + [full(a) for a in params],
        out_specs=[pl.BlockSpec((tm, W), lambda i: (i, 0))] * 10,
        out_shape=[jax.ShapeDtypeStruct((ntot, W), dt) for dt in RWKV_PREP_DTYPES],
        scratch_shapes=[pltpu.VMEM((tm + 2 * HALO, 3 * W), F32), pltpu.VMEM((tm + 2 * HALO, 2 * LANES), F32)],
        compiler_params=_params("parallel"),
        name="rwkv7_prep",
    )(p, p, p, p, p, p, *params)


def _gdn_prep_kernel(x_ref, x_p, x_n, ab_ref, cw_ref, alog_ref, dt_ref, q_o, k_o, v_o, gb_o, ext,
                     *, n_lat, t_len, ctx_len):
    tm = x_ref.shape[0]
    W = GDN_WIDTH
    first, last = _seq_edges(tm, n_lat, t_len, ctx_len)
    _fill_halo(ext, x_ref, x_p, x_n, first, last)
    half = GDN_CONV // 2
    conv = functools.reduce(jnp.add, [ext[HALO - half + j:HALO - half + j + tm, :] * cw_ref[j:j + 1, :]
                                      for j in range(GDN_CONV)])
    act = conv * jax.nn.sigmoid(conv)

    def l2n(x):
        parts = []
        for hd in range(GDN_HEADS):
            xh = x[:, hd * GDN_HEAD:(hd + 1) * GDN_HEAD]
            parts.append(xh * lax.rsqrt(jnp.sum(xh * xh, axis=-1, keepdims=True) + 1e-12))
        return jnp.concatenate(parts, axis=1)

    q_o[...] = (l2n(act[:, :W]) * GDN_HEAD ** -0.5).astype(q_o.dtype)
    k_o[...] = l2n(act[:, W:2 * W]).astype(k_o.dtype)
    v_o[...] = act[:, 2 * W:].astype(v_o.dtype)
    ab = ab_ref[...]
    lane = lax.broadcasted_iota(jnp.int32, ab.shape, 1)
    nh2 = 2 * GDN_HEADS
    glog = -jnp.exp(alog_ref[...]) * _softplus(ab + dt_ref[...])
    gb_o[...] = jnp.where(lane < nh2, glog, jnp.where(lane < 2 * nh2, jax.nn.sigmoid(ab), 0.0))


def _gdn_prep(p, conv_w, a_log, dt_bias, n_lat, t_len, ctx_len, tm):
    ntot = p.shape[0]
    W = GDN_WIDTH
    full = lambda arr: pl.BlockSpec(arr.shape, lambda i: (0,) * arr.ndim)
    return pl.pallas_call(
        functools.partial(_gdn_prep_kernel, n_lat=n_lat, t_len=t_len, ctx_len=ctx_len),
        grid=(ntot // tm,),
        in_specs=_halo_specs(tm, 3 * W, COL_GQKV // (3 * W), ntot)
                 + [pl.BlockSpec((tm, LANES), lambda i: (i, COL_GAB // LANES)), full(conv_w), full(a_log), full(dt_bias)],
        out_specs=[pl.BlockSpec((tm, W), lambda i: (i, 0))] * 3 + [pl.BlockSpec((tm, LANES), lambda i: (i, 0))],
        out_shape=[jax.ShapeDtypeStruct((ntot, W), BF16)] * 3 + [jax.ShapeDtypeStruct((ntot, LANES), F32)],
        scratch_shapes=[pltpu.VMEM((tm + 2 * HALO, 3 * W), F32)],
        compiler_params=_params("parallel"),
        name="gdn_prep",
    )(p, p, p, p, conv_w, a_log, dt_bias)


def kernel(x, c, ctx, c_ctx, w_mod, b_mod, w_in, mla_q_norm, mla_kv_norm, mla_w_uq, mla_w_ukv, rwkv_mu, rwkv_w0, rwkv_w2, rwkv_a0, rwkv_a2, rwkv_g2, rwkv_k_k, rwkv_k_a, rwkv_r_k, rwkv_gn_g, rwkv_gn_b, gdn_conv, gdn_a_log, gdn_dt_bias, gdn_norm, w_out, ln1_g, ln1_b, ffn_w_gate, ffn_w_up, ffn_w_down, ln2_g, ln2_b):
    B, T, D = x.shape
    CTX = ctx.shape[1]
    L = w_mod.shape[0]
    n_lat, n_ctx = B * T, B * CTX
    ntot = n_lat + n_ctx
    alpha = (2.0 * L) ** 0.25
    tm = 256
    tm_big = 512 if (n_lat % 512 == 0 and n_ctx % 512 == 0 and T % 512 == 0) else tm
    tf = 512
    assert T % tm == 0 and n_ctx % tm == 0 and B + 1 <= 8

    cvec = jnp.concatenate([c, c_ctx[None], jnp.zeros((8 - B - 1, D), F32)], axis=0)
    mod_all = _modulation_all(cvec, w_mod, b_mod).reshape(L, 8, 6, D)[:, :B + 1]
    mod_all = jnp.pad(mod_all, ((0, 0), (0, 0), (0, 2), (0, 0)))

    in_idx = _in_proj_columns()
    w_in_p = _take_cols(w_in, in_idx).astype(BF16)
    uq_idx = _uq_columns()
    w_uq_p = _take_cols(mla_w_uq, uq_idx).astype(BF16)
    w_ukv_b = mla_w_ukv.astype(BF16)
    w_out_b = w_out.astype(BF16)
    wg_b, wu_b, wd_b = ffn_w_gate.astype(BF16), ffn_w_up.astype(BF16), ffn_w_down.astype(BF16)
    cs1, cs2 = _rope_tables(T, tm)

    W = RWKV_WIDTH
    mu_rkv = rwkv_mu[:, None, :3 * W]
    mu_lg = jnp.pad(rwkv_mu[:, None, 3 * W:], ((0, 0), (0, 0), (0, 2 * LANES - (rwkv_mu.shape[1] - 3 * W))))
    lw, la = RWKV_LORA_W, RWKV_LORA_A
    w2p = jnp.zeros((L, 2, LANES, W), F32)
    a2p = jnp.zeros((L, 2, LANES, W), F32)
    for d in range(2):
        w2p = w2p.at[:, d, d * lw:(d + 1) * lw].set(rwkv_w2[:, d])
        a2p = a2p.at[:, d, 2 * lw + d * la:2 * lw + (d + 1) * la].set(rwkv_a2[:, d])
    w2p, a2p = w2p.astype(BF16), a2p.astype(BF16)
    g2p = jnp.pad(rwkv_g2, ((0, 0), (0, LANES - RWKV_LORA_G), (0, 0))).astype(BF16)
    lane_head = np.arange(W) // RWKV_HEAD
    seg = jnp.asarray(lane_head[:, None] == lane_head[None, :], BF16)
    conv_p = jnp.pad(gdn_conv, ((0, 0), (0, 8 - GDN_CONV), (0, 0)))
    nh2 = 2 * GDN_HEADS
    alog_p = jnp.pad(gdn_a_log.reshape(L, 1, nh2), ((0, 0), (0, 0), (0, LANES - nh2)))
    dt_p = jnp.pad(gdn_dt_bias.reshape(L, 1, nh2), ((0, 0), (0, 0), (0, LANES - nh2)))
    r_k = rwkv_r_k.reshape(L, 1, W)
    gdn_norm_t = jnp.tile(gdn_norm, (1, GDN_HEADS))[:, None, :]

    hs = (x.reshape(n_lat, D), ctx.reshape(n_ctx, D))
    for l in range(L):
        last = l == L - 1
        mod = mod_all[l]
        p = _in_proj(hs, mod, w_in_p[l], n_lat, T, B, tm_big)

        qh, kh, vh = _mla_proj(p, mla_q_norm[l][None], mla_kv_norm[l][None], w_uq_p[l], w_ukv_b[l], cs1, cs2,
                               n_lat, T, tm)
        a_mix = _attention(qh, kh, vh, B, T, CTX, with_ctx=not last)

        r, v, a, lw0, lw1, kd0, kd1, b0, b1, g = _rwkv_prep(
            p, mu_rkv[l], mu_lg[l], rwkv_w0[l], rwkv_a0[l], w2p[l], a2p[l], g2p[l],
            rwkv_k_k[l][None], rwkv_k_a[l][None], seg, n_lat, T, CTX, tm)
        yf, yb = _rwkv_scan(r, v, a, lw0, lw1, kd0, kd1, b0, b1, B, T, CTX)

        gq, gk, gv, gb = _gdn_prep(p, conv_p[l], alog_p[l], dt_p[l], n_lat, T, CTX, tm)
        of, ob = _gdn_scan(gq, gk, gv, gb, B, T, CTX)

        n_rows = n_lat if last else ntot
        h1 = _out_proj(hs, a_mix, (yf, yb, r, v, kd0, kd1, g), (of, ob), p, r_k[l], rwkv_gn_g[l][None],
                       rwkv_gn_b[l][None], gdn_norm_t[l], seg, w_out_b[l], mod, ln1_g[l][None], ln1_b[l][None],
                       n_rows, n_lat, T, B, tm, alpha)
        hs = (_ffn(h1, mod, wg_b[l], wu_b[l], wd_b[l], ln2_g[l][None], ln2_b[l][None],
                   n_rows, n_lat, T, B, tm_big, tf, alpha),)
    return hs[0][:n_lat].reshape(B, T, D)
```
